```python
import math
import jax, jax.numpy as jnp
from jax import lax
import numpy as np

D_MODEL = 2048
BATCH = 4
SEQ = 4096
DEPTH = 1
DEC_BATCH = 4
DEC_SEQ = 2048
PAST_LEN = 128

HEAD_DIM = 128
N_Q_HEADS = 8
N_KV_HEADS = 2
Q_PER_KV = N_Q_HEADS // N_KV_HEADS
ATTN_WIDTH = N_Q_HEADS * HEAD_DIM
KV_WIDTH = N_KV_HEADS * HEAD_DIM
WINDOW = 128
BLOCK = 128
ROPE_THETA = 10000.0
SSM_WIDTH = D_MODEL // 2
SSM_GROUP = 16
SSM_GROUPS = SSM_WIDTH // SSM_GROUP
SSM_STATE = 64
DT_MIN = 1e-3
DT_MAX = 1e-1
D_FF = 5632
IN_WIDTH = ATTN_WIDTH + 2 * KV_WIDTH + SSM_WIDTH + 2 * D_MODEL
SPLITS = (ATTN_WIDTH, ATTN_WIDTH + KV_WIDTH, ATTN_WIDTH + 2 * KV_WIDTH,
          ATTN_WIDTH + 2 * KV_WIDTH + SSM_WIDTH, ATTN_WIDTH + 2 * KV_WIDTH + SSM_WIDTH + D_MODEL)
EPS = 1e-6

kernel_name = 'hybrid_bidir_swa_s5_macaron_encoder'


def rmsnorm(x, gain):
    xf = x.astype(jnp.float32)
    y = xf * lax.rsqrt(jnp.mean(xf * xf, axis=-1, keepdims=True) + EPS)
    return (y * gain.astype(jnp.float32)).astype(x.dtype)


def swiglu(x, w_gate_up, w_down):
    gate, up = jnp.split(x @ w_gate_up, 2, axis=-1)
    return (jax.nn.silu(gate) * up) @ w_down


def rope(x, pos):
    d = x.shape[-1]
    inv_freq = ROPE_THETA ** (-jnp.arange(0, d, 2, dtype=jnp.float32) / d)
    ang = pos.astype(jnp.float32)[:, None] * inv_freq[None, :]
    cos = jnp.cos(ang)[:, None, :]
    sin = jnp.sin(ang)[:, None, :]
    xf = x.astype(jnp.float32)
    x1, x2 = jnp.split(xf, 2, axis=-1)
    out = jnp.concatenate([x1 * cos - x2 * sin, x1 * sin + x2 * cos], axis=-1)
    return out.astype(x.dtype)


def windowed_gqa(q, k, v, sink):
    b, s, _, d = q.shape
    nb = s // BLOCK
    qb = q.reshape(b, nb, BLOCK, N_KV_HEADS, Q_PER_KV, d)
    pad = ((0, 0), (BLOCK, BLOCK), (0, 0), (0, 0))

    def band(t):
        tb = jnp.pad(t, pad).reshape(b, nb + 2, BLOCK, N_KV_HEADS, d)
        return jnp.concatenate([tb[:, :-2], tb[:, 1:-1], tb[:, 2:]], axis=2)

    kw = band(k)
    vw = band(v)
    scores = jnp.einsum('bnqhgd,bnkhd->bnhgqk', qb, kw,
                        preferred_element_type=jnp.float32) * (d ** -0.5)
    qi = jnp.arange(BLOCK)[:, None]
    kj = jnp.arange(3 * BLOCK)[None, :]
    rel_ok = jnp.abs(kj - BLOCK - qi) <= WINDOW
    kpos = jnp.arange(nb)[:, None] * BLOCK - BLOCK + jnp.arange(3 * BLOCK)[None, :]
    in_range = (kpos >= 0) & (kpos < s)
    mask = rel_ok[None, :, :] & in_range[:, None, :]
    scores = jnp.where(mask[None, :, None, None], scores, -jnp.inf)
    sink_l = sink.astype(jnp.float32).reshape(N_KV_HEADS, Q_PER_KV)[None, None, :, :, None]
    m = jnp.maximum(jnp.max(scores, axis=-1), sink_l)
    p = jnp.exp(scores - m[..., None])
    denom = jnp.sum(p, axis=-1) + jnp.exp(sink_l - m)
    probs = (p / denom[..., None]).astype(v.dtype)
    out = jnp.einsum('bnhgqk,bnkhd->bnqhgd', probs, vw)
    return out.reshape(b, s, N_Q_HEADS * HEAD_DIM)


def _ssm_combine(e1, e2):
    a1, b1 = e1
    a2, b2 = e2
    return a1 * a2, a2 * b1 + b2


def s5_bidirectional(u, lam_re, lam_im, log_dt, b_re, b_im, c_re, c_im, d_skip):
    bsz, s, _ = u.shape
    uf = u.astype(jnp.float32).reshape(bsz, s, SSM_GROUPS, SSM_GROUP)
    uc = uf.astype(jnp.complex64)
    y = d_skip.astype(jnp.float32).reshape(SSM_GROUPS, SSM_GROUP) * uf
    for direction in range(2):
        lam = lax.complex(lam_re[direction].astype(jnp.float32), lam_im[direction].astype(jnp.float32))
        dt = jnp.exp(log_dt[direction].astype(jnp.float32))[:, None]
        lam_bar = jnp.exp(lam * dt)
        bmat = lax.complex(b_re[direction].astype(jnp.float32), b_im[direction].astype(jnp.float32))
        b_bar = ((lam_bar - 1.0) / lam)[..., None] * bmat
        bu = jnp.einsum('bsgc,gpc->bsgp', uc, b_bar)
        a = jnp.broadcast_to(lam_bar, bu.shape)
        _, states = lax.associative_scan(_ssm_combine, (a, bu), reverse=(direction == 1), axis=1)
        cmat = lax.complex(c_re[direction].astype(jnp.float32), c_im[direction].astype(jnp.float32))
        y = y + jnp.real(jnp.einsum('bsgp,gcp->bsgc', states, cmat))
    return y.reshape(bsz, s, SSM_WIDTH).astype(u.dtype)


def encoder_layer(x, ffn1_norm, ffn1_w_gate_up, ffn1_w_down, mix_norm, w_in, q_norm, k_norm, attn_sink,
                  ssm_lambda_re, ssm_lambda_im, ssm_log_dt, ssm_b_re, ssm_b_im, ssm_c_re, ssm_c_im, ssm_d,
                  w_glu, b_glu, w_attn_out, w_ssm_out, w_out, ffn2_norm, ffn2_w_gate_up, ffn2_w_down):
    b, s, _ = x.shape
    x = x + 0.5 * swiglu(rmsnorm(x, ffn1_norm), ffn1_w_gate_up, ffn1_w_down)
    h = rmsnorm(x, mix_norm)
    q, k, v, u, g_attn, g_ssm = jnp.split(h @ w_in, SPLITS, axis=-1)
    pos = jnp.arange(s)
    q = rope(rmsnorm(q.reshape(b, s, N_Q_HEADS, HEAD_DIM), q_norm), pos)
    k = rope(rmsnorm(k.reshape(b, s, N_KV_HEADS, HEAD_DIM), k_norm), pos)
    v = v.reshape(b, s, N_KV_HEADS, HEAD_DIM)
    attn_branch = windowed_gqa(q, k, v, attn_sink) @ w_attn_out
    z = jax.nn.gelu(s5_bidirectional(u, ssm_lambda_re, ssm_lambda_im, ssm_log_dt,
                                     ssm_b_re, ssm_b_im, ssm_c_re, ssm_c_im, ssm_d))
    z = z * jax.nn.sigmoid(z @ w_glu + b_glu)
    ssm_branch = z @ w_ssm_out
    merged = jax.nn.sigmoid(g_attn) * attn_branch + jax.nn.sigmoid(g_ssm) * ssm_branch
    x = x + merged @ w_out
    x = x + 0.5 * swiglu(rmsnorm(x, ffn2_norm), ffn2_w_gate_up, ffn2_w_down)
    return x


def setup_inputs(seed: int = 0) -> dict:
    key = jax.random.key(seed)
    ks = jax.random.split(key, 32)
    f32 = jnp.float32
    L = DEPTH

    def nrm(k, shape, scale):
        return jax.random.normal(k, shape, f32) * scale

    def gain(k, shape):
        return 1.0 + 0.02 * jax.random.normal(k, shape, f32)

    n_idx = jnp.arange(SSM_STATE, dtype=f32)
    lam_re = -0.5 + 0.01 * jax.random.normal(ks[10], (L, 2, SSM_GROUPS, SSM_STATE), f32)
    lam_im = math.pi * n_idx + 0.01 * jax.random.normal(ks[11], (L, 2, SSM_GROUPS, SSM_STATE), f32)
    log_dt = jax.random.uniform(ks[12], (L, 2, SSM_GROUPS), f32, math.log(DT_MIN), math.log(DT_MAX))
    b_scale = (2.0 * SSM_GROUP) ** -0.5
    c_scale = (2.0 * SSM_STATE) ** -0.5
    return {
        'x_prompt': jax.random.normal(ks[0], (BATCH, SEQ, D_MODEL), f32),
        'x_sample': jax.random.normal(ks[1], (DEC_BATCH, DEC_SEQ, D_MODEL), f32),
        'ffn1_norm': gain(ks[2], (L, D_MODEL)),
        'ffn1_w_gate_up': nrm(ks[3], (L, D_MODEL, 2 * D_FF), D_MODEL ** -0.5),
        'ffn1_w_down': nrm(ks[4], (L, D_FF, D_MODEL), D_FF ** -0.5),
        'mix_norm': gain(ks[5], (L, D_MODEL)),
        'w_in': nrm(ks[6], (L, D_MODEL, IN_WIDTH), D_MODEL ** -0.5),
        'q_norm': gain(ks[7], (L, HEAD_DIM)),
        'k_norm': gain(ks[8], (L, HEAD_DIM)),
        'attn_sink': nrm(ks[9], (L, N_Q_HEADS), 0.5),
        'ssm_lambda_re': lam_re,
        'ssm_lambda_im': lam_im,
        'ssm_log_dt': log_dt,
        'ssm_b_re': nrm(ks[13], (L, 2, SSM_GROUPS, SSM_STATE, SSM_GROUP), b_scale),
        'ssm_b_im': nrm(ks[14], (L, 2, SSM_GROUPS, SSM_STATE, SSM_GROUP), b_scale),
        'ssm_c_re': nrm(ks[15], (L, 2, SSM_GROUPS, SSM_GROUP, SSM_STATE), c_scale),
        'ssm_c_im': nrm(ks[16], (L, 2, SSM_GROUPS, SSM_GROUP, SSM_STATE), c_scale),
        'ssm_d': nrm(ks[17], (L, SSM_WIDTH), 1.0),
        'w_glu': nrm(ks[18], (L, SSM_WIDTH, SSM_WIDTH), SSM_WIDTH ** -0.5),
        'b_glu': nrm(ks[19], (L, SSM_WIDTH), 0.01),
        'w_attn_out': nrm(ks[20], (L, ATTN_WIDTH, D_MODEL), ATTN_WIDTH ** -0.5),
        'w_ssm_out': nrm(ks[21], (L, SSM_WIDTH, D_MODEL), SSM_WIDTH ** -0.5),
        'w_out': nrm(ks[22], (L, D_MODEL, D_MODEL), D_MODEL ** -0.5),
        'ffn2_norm': gain(ks[23], (L, D_MODEL)),
        'ffn2_w_gate_up': nrm(ks[24], (L, D_MODEL, 2 * D_FF), D_MODEL ** -0.5),
        'ffn2_w_down': nrm(ks[25], (L, D_FF, D_MODEL), D_FF ** -0.5),
    }


def reference(x_prompt, x_sample, ffn1_norm, ffn1_w_gate_up, ffn1_w_down, mix_norm, w_in, q_norm, k_norm,
              attn_sink, ssm_lambda_re, ssm_lambda_im, ssm_log_dt, ssm_b_re, ssm_b_im, ssm_c_re, ssm_c_im,
              ssm_d, w_glu, b_glu, w_attn_out, w_ssm_out, w_out, ffn2_norm, ffn2_w_gate_up, ffn2_w_down):
    y_prompt = x_prompt
    y_sample = x_sample
    for l in range(DEPTH):
        params = (ffn1_norm[l], ffn1_w_gate_up[l], ffn1_w_down[l], mix_norm[l], w_in[l], q_norm[l], k_norm[l],
                  attn_sink[l], ssm_lambda_re[l], ssm_lambda_im[l], ssm_log_dt[l], ssm_b_re[l], ssm_b_im[l],
                  ssm_c_re[l], ssm_c_im[l], ssm_d[l], w_glu[l], b_glu[l], w_attn_out[l], w_ssm_out[l], w_out[l],
                  ffn2_norm[l], ffn2_w_gate_up[l], ffn2_w_down[l])
        y_prompt = encoder_layer(y_prompt, *params)
        y_sample = encoder_layer(y_sample, *params)
    return (y_prompt, y_sample)
```

```python
import functools
import math

import jax
import jax.numpy as jnp
from jax import lax
from jax.experimental import pallas as pl
from jax.experimental.pallas import tpu as pltpu

F32 = jnp.float32
BF16 = jnp.bfloat16

HEAD_DIM = 128
N_Q_HEADS = 8
N_KV_HEADS = 2
Q_PER_KV = N_Q_HEADS // N_KV_HEADS
WINDOW = 128
ATTN_BLOCK = 128
ROPE_THETA = 10000.0
SSM_GROUP = 16
SSM_STATE = 64
EPS = 1e-6

LANES = 128
VMEM_LIMIT_CAP = 56 * 1024 * 1024

SSM_CHUNK = 64
CHUNK_W = SSM_CHUNK * SSM_GROUP
FFN_ROWS = 512
FFN_COLS = 512
INPROJ_ROWS = 512
MERGE_ROWS = 256


def _cparams(semantics, vmem_bytes):
    return pltpu.CompilerParams(dimension_semantics=semantics,
                                vmem_limit_bytes=min(int(vmem_bytes), VMEM_LIMIT_CAP))


def _rms(x, gain):
    return x * lax.rsqrt(jnp.mean(x * x, axis=-1, keepdims=True) + EPS) * gain


def _dot(a, b):
    return jnp.dot(a, b, preferred_element_type=F32)


def _dot_exact(a, b):
    return jnp.dot(a, b, preferred_element_type=F32, precision=lax.Precision.HIGHEST)


def _ffn_kernel(x_ref, gain_ref, wg_ref, wu_ref, wd_ref, o_ref, h_ref):
    f = pl.program_id(1)

    @pl.when(f == 0)
    def _():
        h_ref[...] = _rms(x_ref[...], gain_ref[...]).astype(BF16)
        o_ref[...] = jnp.zeros_like(o_ref)

    h = h_ref[...]
    g = _dot(h, wg_ref[...])
    u = _dot(h, wu_ref[...])
    a = (g * jax.nn.sigmoid(g)) * u
    o_ref[...] += _dot(a.astype(BF16), wd_ref[...])

    @pl.when(f == pl.num_programs(1) - 1)
    def _():
        o_ref[...] = x_ref[...] + 0.5 * o_ref[...]


def _ffn(x, gain, w_gate_up, w_down):
    t, d = x.shape
    d_ff = w_down.shape[0]
    tm, tf = FFN_ROWS, FFN_COLS
    nf = d_ff // tf
    vmem = (4 * tm * d * 4) + (tm * d * 2) + 2 * 3 * (d * tf * 2) + 4 * (tm * tf * 4) + (4 << 20)
    return pl.pallas_call(
        _ffn_kernel,
        out_shape=jax.ShapeDtypeStruct((t, d), F32),
        grid=(t // tm, nf),
        in_specs=[
            pl.BlockSpec((tm, d), lambda i, f: (i, 0)),
            pl.BlockSpec((1, d), lambda i, f: (0, 0)),
            pl.BlockSpec((d, tf), lambda i, f: (0, f)),
            pl.BlockSpec((d, tf), lambda i, f: (0, f + nf)),
            pl.BlockSpec((tf, d), lambda i, f: (f, 0)),
        ],
        out_specs=pl.BlockSpec((tm, d), lambda i, f: (i, 0)),
        scratch_shapes=[pltpu.VMEM((tm, d), BF16)],
        compiler_params=_cparams(("parallel", "arbitrary"), vmem),
        name="ffn",
    )(x, gain.reshape(1, d), w_gate_up, w_gate_up, w_down)


def _inproj_kernel(x_ref, gain_ref, w_ref, o_ref):
    h = _rms(x_ref[...], gain_ref[...]).astype(BF16)
    o_ref[...] = _dot(h, w_ref[...]).astype(BF16)


def _inproj(x, gain, w):
    t, d = x.shape
    n = w.shape[1]
    tm = INPROJ_ROWS
    n_col_tiles = 2
    tn = n // n_col_tiles
    vmem = 2 * (tm * d * 4) + 2 * (d * tn * 2) + 2 * (tm * tn * 2) + (tm * tn * 4) + (tm * d * 6) + (4 << 20)
    return pl.pallas_call(
        _inproj_kernel,
        out_shape=jax.ShapeDtypeStruct((t, n), BF16),
        grid=(n_col_tiles, t // tm),
        in_specs=[
            pl.BlockSpec((tm, d), lambda j, i: (i, 0)),
            pl.BlockSpec((1, d), lambda j, i: (0, 0)),
            pl.BlockSpec((d, tn), lambda j, i: (0, j)),
        ],
        out_specs=pl.BlockSpec((tm, tn), lambda j, i: (i, j)),
        compiler_params=_cparams(("parallel", "parallel"), vmem),
        name="inproj",
    )(x, gain.reshape(1, d), w)


def _rope_table_kernel(freq_ref, cos_ref, sin_ref):
    rows = cos_ref.shape[0]
    pos = (lax.broadcasted_iota(jnp.int32, (rows, HEAD_DIM), 0) + pl.program_id(0) * rows).astype(F32)
    ang = pos * freq_ref[...]
    lane = lax.broadcasted_iota(jnp.int32, (rows, HEAD_DIM), 1)
    cos_ref[...] = jnp.cos(ang)
    sin_ref[...] = jnp.where(lane < HEAD_DIM // 2, -1.0, 1.0) * jnp.sin(ang)


def _rope_tables(seq):
    half = HEAD_DIM // 2
    inv_freq = ROPE_THETA ** (-jnp.arange(0, HEAD_DIM, 2, dtype=F32) / HEAD_DIM)
    freq2 = jnp.concatenate([inv_freq, inv_freq]).reshape(1, HEAD_DIM)
    rows = math.gcd(seq, 512)
    assert half * 2 == HEAD_DIM
    return pl.pallas_call(
        _rope_table_kernel,
        out_shape=(jax.ShapeDtypeStruct((seq, HEAD_DIM), F32),) * 2,
        grid=(seq // rows,),
        in_specs=[pl.BlockSpec((1, HEAD_DIM), lambda i: (0, 0))],
        out_specs=(pl.BlockSpec((rows, HEAD_DIM), lambda i: (i, 0)),) * 2,
        compiler_params=_cparams(("parallel",), 16 << 20),
        name="rope_tables",
    )(freq2)


def _rope(x, cos2, sin2):
    return x * cos2 + pltpu.roll(x, HEAD_DIM // 2, 1) * sin2


def _attn_kernel(sink_ref, q_ref, kp_ref, kc_ref, kn_ref, vp_ref, vc_ref, vn_ref,
                 cp_ref, cc_ref, cn_ref, sp_ref, sc_ref, sn_ref, qg_ref, kg_ref, o_ref):
    n = pl.program_id(1)
    nb = pl.num_programs(1)
    blk = ATTN_BLOCK
    qg = qg_ref[...] * (HEAD_DIM ** -0.5)
    kg = kg_ref[...]
    cos_q, sin_q = cc_ref[...], sc_ref[...]

    rows = Q_PER_KV * blk
    qi = lax.broadcasted_iota(jnp.int32, (rows, 3 * blk), 0) & (blk - 1)
    kj = lax.broadcasted_iota(jnp.int32, (rows, 3 * blk), 1)
    rel = kj - blk - qi
    ok = (rel >= -WINDOW) & (rel <= WINDOW)
    ok = ok & ((kj >= blk) | (n > 0)) & ((kj < 2 * blk) | (n < nb - 1))

    k_tabs = ((kp_ref, cp_ref, sp_ref), (kc_ref, cc_ref, sc_ref), (kn_ref, cn_ref, sn_ref))
    for h in range(N_KV_HEADS):
        cols = slice(h * HEAD_DIM, (h + 1) * HEAD_DIM)
        k_band = jnp.concatenate(
            [_rope(_rms(k_ref[:, cols].astype(F32), kg), c_ref[...], s_ref[...]).astype(BF16)
             for k_ref, c_ref, s_ref in k_tabs], axis=0)
        v_band = jnp.concatenate([vp_ref[:, cols], vc_ref[:, cols], vn_ref[:, cols]], axis=0)
        q_rows, sink_rows = [], []
        for g in range(Q_PER_KV):
            head = h * Q_PER_KV + g
            qh = q_ref[:, head * HEAD_DIM:(head + 1) * HEAD_DIM].astype(F32)
            q_rows.append(_rope(_rms(qh, qg), cos_q, sin_q).astype(BF16))
            sink_rows.append(jnp.full((blk, 1), sink_ref[head], F32))
        q_all = jnp.concatenate(q_rows, axis=0)
        sink = jnp.concatenate(sink_rows, axis=0)
        s = lax.dot_general(q_all, k_band, (((1,), (1,)), ((), ())), preferred_element_type=F32)
        s = jnp.where(ok, s, -jnp.inf)
        m = jnp.maximum(jnp.max(s, axis=-1, keepdims=True), sink)
        p = jnp.exp(s - m)
        denom = jnp.sum(p, axis=-1, keepdims=True) + jnp.exp(sink - m)
        o = _dot(p.astype(BF16), v_band) / denom
        for g in range(Q_PER_KV):
            head = h * Q_PER_KV + g
            o_ref[:, head * HEAD_DIM:(head + 1) * HEAD_DIM] = o[g * blk:(g + 1) * blk].astype(BF16)


def _attention(p, row0, batch, seq, col_q, col_k, col_v, cos2, sin2, q_gain, k_gain, sink):
    blk = ATTN_BLOCK
    nb = seq // blk
    base = row0 // blk
    qw = N_Q_HEADS * HEAD_DIM
    kw = N_KV_HEADS * HEAD_DIM
    assert col_q % qw == 0 and col_k % kw == 0 and col_v % kw == 0 and row0 % blk == 0

    def rowblk(b, n, *_):
        return base + b * nb + n

    def prev(n):
        return jnp.maximum(n - 1, 0)

    def nxt(n):
        return jnp.minimum(n + 1, nb - 1)

    def kv_spec(col, shift):
        return pl.BlockSpec((blk, kw), lambda b, n, s: (rowblk(b, shift(n)), col // kw))

    def tab_spec(shift):
        return pl.BlockSpec((blk, HEAD_DIM), lambda b, n, s: (shift(n), 0))

    ident = lambda n: n
    grid_spec = pltpu.PrefetchScalarGridSpec(
        num_scalar_prefetch=1,
        grid=(batch, nb),
        in_specs=[
            pl.BlockSpec((blk, qw), lambda b, n, s: (rowblk(b, n), col_q // qw)),
            kv_spec(col_k, prev), kv_spec(col_k, ident), kv_spec(col_k, nxt),
            kv_spec(col_v, prev), kv_spec(col_v, ident), kv_spec(col_v, nxt),
            tab_spec(prev), tab_spec(ident), tab_spec(nxt),
            tab_spec(prev), tab_spec(ident), tab_spec(nxt),
            pl.BlockSpec((1, HEAD_DIM), lambda b, n, s: (0, 0)),
            pl.BlockSpec((1, HEAD_DIM), lambda b, n, s: (0, 0)),
        ],
        out_specs=pl.BlockSpec((blk, qw), lambda b, n, s: (b * nb + n, 0)),
    )
    return pl.pallas_call(
        _attn_kernel,
        out_shape=jax.ShapeDtypeStruct((batch * seq, qw), BF16),
        grid_spec=grid_spec,
        compiler_params=_cparams(("parallel", "parallel"), 32 << 20),
        name="attention",
    )(sink, p, p, p, p, p, p, p, cos2, cos2, cos2, sin2, sin2, sin2,
      q_gain.reshape(1, HEAD_DIM), k_gain.reshape(1, HEAD_DIM))


def _cmul(ar, ai, br, bi):
    return ar * br - ai * bi, ar * bi + ai * br


def _cpow_table(br, bi, expo, nbits):
    pr = jnp.ones(expo.shape, F32)
    pi = jnp.zeros(expo.shape, F32)
    br = jnp.broadcast_to(br, expo.shape)
    bi = jnp.broadcast_to(bi, expo.shape)
    for k in range(nbits):
        bit = ((expo >> k) & 1) == 1
        nr, ni = _cmul(pr, pi, br, bi)
        pr = jnp.where(bit, nr, pr)
        pi = jnp.where(bit, ni, pi)
        if k + 1 < nbits:
            br, bi = _cmul(br, bi, br, bi)
    return pr, pi


def _lambda_bar(lre, lim, log_dt):
    dt = jnp.exp(log_dt)
    mag = jnp.exp(lre * dt)
    return mag * jnp.cos(lim * dt), mag * jnp.sin(lim * dt)


def _ssm_prep_kernel(rowp_ref, colp_ref, btr_ref, bti_ref, ctr_ref, cti_ref, dskip_ref,
                     t_ref, wz_ref, wy_ref, dec_ref):
    L, P, C = SSM_CHUNK, SSM_STATE, SSM_GROUP
    log2c = C.bit_length() - 1
    nbits = L.bit_length()

    lre, lim, ldt = rowp_ref[0, 0:1, :], rowp_ref[0, 1:2, :], rowp_ref[0, 2:3, :]
    lbr, lbi = _lambda_bar(lre, lim, ldt)
    den = lre * lre + lim * lim
    cfr = ((lbr - 1.0) * lre + lbi * lim) / den
    cfi = (lbi * lre - (lbr - 1.0) * lim) / den
    bbr, bbi = _cmul(cfr, cfi, btr_ref[0], bti_ref[0])

    dr, di = lbr, lbi
    for _ in range(L.bit_length() - 1):
        dr, di = _cmul(dr, di, dr, di)
    dec_ref[0] = jnp.concatenate([dr, di, jnp.zeros((6, 2 * P), F32)], axis=0)

    n_sub = lax.broadcasted_iota(jnp.int32, (L, 2 * P), 0)
    lane = lax.broadcasted_iota(jnp.int32, (L, 2 * P), 1)
    pzr, pzi = _cpow_table(lbr, lbi, jnp.where(lane < P, L - 1 - n_sub, n_sub), nbits)
    r_idx = lax.broadcasted_iota(jnp.int32, (CHUNK_W, L), 0)
    rep_rows_n = ((r_idx >> log2c) == lax.broadcasted_iota(jnp.int32, (CHUNK_W, L), 1)).astype(F32)
    rc_idx = lax.broadcasted_iota(jnp.int32, (CHUNK_W, C), 0)
    rep_rows_c = ((rc_idx & (C - 1)) == lax.broadcasted_iota(jnp.int32, (CHUNK_W, C), 1)).astype(F32)
    zr, zi = _cmul(_dot_exact(rep_rows_n, pzr), _dot_exact(rep_rows_n, pzi),
                   _dot_exact(rep_rows_c, bbr), _dot_exact(rep_rows_c, bbi))
    wz_ref[0, :, 0:2 * P] = zr.astype(BF16)
    wz_ref[0, :, 2 * P:4 * P] = zi.astype(BF16)

    clre, clim, cldt = colp_ref[0, :, 0:1], colp_ref[0, :, 1:2], colp_ref[0, :, 2:3]
    cbr, cbi = _lambda_bar(clre, clim, cldt)
    n_lane = lax.broadcasted_iota(jnp.int32, (2 * P, LANES), 1)
    pwr, pwi = _cpow_table(cbr, cbi, n_lane, nbits)

    def expand_pow(width, n_fwd, n_bwd):
        blk_idx = lax.broadcasted_iota(jnp.int32, (LANES, width), 1) >> log2c
        n_row = lax.broadcasted_iota(jnp.int32, (LANES, width), 0)
        rep_f = (n_fwd(blk_idx) == n_row).astype(F32)
        rep_b = (n_bwd(blk_idx) == n_row).astype(F32)
        er = jnp.concatenate([_dot_exact(pwr[0:P], rep_f), _dot_exact(pwr[P:2 * P], rep_b)], axis=0)
        ei = jnp.concatenate([_dot_exact(pwi[0:P], rep_f), _dot_exact(pwi[P:2 * P], rep_b)], axis=0)
        return er, ei

    def expand_c(width):
        c_idx = lax.broadcasted_iota(jnp.int32, (C, width), 1) & (C - 1)
        rep = (c_idx == lax.broadcasted_iota(jnp.int32, (C, width), 0)).astype(F32)
        return _dot_exact(ctr_ref[0], rep), _dot_exact(cti_ref[0], rep)

    er, ei = expand_pow(CHUNK_W, lambda b: b + 1, lambda b: L - b)
    cr, ci = expand_c(CHUNK_W)
    yr, yi = _cmul(cr, ci, er, ei)
    wy_ref[0, 0:2 * P, :] = yr.astype(BF16)
    wy_ref[0, 2 * P:4 * P, :] = (-yi).astype(BF16)

    er, ei = expand_pow(2 * CHUNK_W, lambda b: b - (L - 1), lambda b: (L - 1) - b)
    cr, ci = expand_c(2 * CHUNK_W)
    kr, ki = _cmul(cr, ci, er, ei)
    w = _dot_exact(bbr, kr) - _dot_exact(bbi, ki)
    q_idx = lax.broadcasted_iota(jnp.int32, (C, 2 * CHUNK_W), 1)
    c_in = lax.broadcasted_iota(jnp.int32, (C, 2 * CHUNK_W), 0)
    on_diag = ((q_idx >> log2c) == L - 1) & ((q_idx & (C - 1)) == c_in)
    w = w + jnp.where(on_diag, dskip_ref[0], 0.0)

    def put_rows(i, carry):
        shift = (L - 1 - i) * C
        win = pltpu.roll(w, jnp.where(shift == 0, 0, 2 * CHUNK_W - shift), 1)[:, 0:CHUNK_W]
        t_ref[0, pl.ds(pl.multiple_of(i * C, C), C), :] = win.astype(BF16)
        return carry

    lax.fori_loop(0, L, put_rows, 0)


def _ssm_prep(lam_re, lam_im, log_dt, b_re, b_im, c_re, c_im, d_skip):
    _, groups, P = lam_re.shape
    C = SSM_GROUP
    assert P == SSM_STATE and 2 * P == LANES and b_re.shape[-1] == C

    def lanes_fb(a):
        return jnp.transpose(a, (1, 0, 2)).reshape(groups, 2 * P)

    ldt = jnp.broadcast_to(log_dt[:, :, None], (2, groups, P))
    rowp = jnp.stack([lanes_fb(lam_re), lanes_fb(lam_im), lanes_fb(ldt)], axis=1)
    colp = jnp.transpose(rowp, (0, 2, 1))
    bt = lambda b: jnp.transpose(b, (1, 3, 0, 2)).reshape(groups, C, 2 * P)
    ct = lambda c: jnp.transpose(c, (1, 0, 3, 2)).reshape(groups, 2 * P, C)
    dsk = jnp.tile(d_skip.reshape(groups, 1, C), (1, 1, 2 * SSM_CHUNK))

    g3 = lambda s1, s2: pl.BlockSpec((1, s1, s2), lambda g: (g, 0, 0))
    return pl.pallas_call(
        _ssm_prep_kernel,
        out_shape=(jax.ShapeDtypeStruct((groups, CHUNK_W, CHUNK_W), BF16),
                   jax.ShapeDtypeStruct((groups, CHUNK_W, 4 * P), BF16),
                   jax.ShapeDtypeStruct((groups, 4 * P, CHUNK_W), BF16),
                   jax.ShapeDtypeStruct((groups, 8, 2 * P), F32)),
        grid=(groups,),
        in_specs=[g3(3, 2 * P), g3(2 * P, 3), g3(C, 2 * P), g3(C, 2 * P), g3(2 * P, C), g3(2 * P, C),
                  g3(1, 2 * CHUNK_W)],
        out_specs=(g3(CHUNK_W, CHUNK_W), g3(CHUNK_W, 4 * P), g3(4 * P, CHUNK_W), g3(8, 2 * P)),
        compiler_params=_cparams(("parallel",), 40 << 20),
        name="ssm_prep",
    )(rowp, colp, bt(b_re), bt(b_im), ct(c_re), ct(c_im), dsk)


def _ssm_kernel(u_ref, t_ref, wz_ref, wy_ref, dec_ref, y_ref, zr_ref, zi_ref, fr_ref, fi_ref, rr_ref, ri_ref,
                *, nseq, nchunks):
    P = SSM_STATE
    u = u_ref[0]
    z = _dot(u, wz_ref[0])
    zr_ref[...] = z[:, 0:2 * P]
    zi_ref[...] = z[:, 2 * P:4 * P]
    ar, ai = dec_ref[0, 0:1, :], dec_ref[0, 1:2, :]
    is_fwd = lax.broadcasted_iota(jnp.int32, (nseq, 2 * P), 1) < P

    def step(t, carry):
        sr, si = carry
        rf = pl.multiple_of(t * nseq, nseq)
        rb = pl.multiple_of((nchunks - 1 - t) * nseq, nseq)
        fr_ref[pl.ds(rf, nseq), :] = sr
        fi_ref[pl.ds(rf, nseq), :] = si
        rr_ref[pl.ds(rb, nseq), :] = sr
        ri_ref[pl.ds(rb, nseq), :] = si
        zr = jnp.where(is_fwd, zr_ref[pl.ds(rf, nseq), :], zr_ref[pl.ds(rb, nseq), :])
        zi = jnp.where(is_fwd, zi_ref[pl.ds(rf, nseq), :], zi_ref[pl.ds(rb, nseq), :])
        return ar * sr - ai * si + zr, ar * si + ai * sr + zi

    zero = jnp.zeros((nseq, 2 * P), F32)
    lax.fori_loop(0, nchunks, step, (zero, zero))

    sel = lax.broadcasted_iota(jnp.int32, fr_ref.shape, 1) < P
    s_in = jnp.concatenate([jnp.where(sel, fr_ref[...], rr_ref[...]),
                            jnp.where(sel, fi_ref[...], ri_ref[...])], axis=1).astype(BF16)
    y_ref[0] = (_dot(u, t_ref[0]) + _dot(s_in, wy_ref[0])).astype(y_ref.dtype)


def _ssm(uv, t_mat, wz, wy, dec, nseq, nchunks):
    groups, rows, w = uv.shape
    P = SSM_STATE
    g3 = lambda s1, s2: pl.BlockSpec((1, s1, s2), lambda g: (g, 0, 0))
    return pl.pallas_call(
        functools.partial(_ssm_kernel, nseq=nseq, nchunks=nchunks),
        out_shape=jax.ShapeDtypeStruct((groups, rows, w), BF16),
        grid=(groups,),
        in_specs=[g3(rows, w), g3(w, w), g3(w, 4 * P), g3(4 * P, w), g3(8, 2 * P)],
        out_specs=g3(rows, w),
        scratch_shapes=[pltpu.VMEM((rows, 2 * P), F32)] * 6,
        compiler_params=_cparams(("parallel",), 40 << 20),
        name="ssm_chunks",
    )(uv, t_mat, wz, wy, dec)


def _to_chunks(u, seqs):
    L, C = SSM_CHUNK, SSM_GROUP
    groups = u.shape[1] // C
    kmax = max(s // L for _, s in seqs)
    parts, row = [], 0
    for b, s in seqs:
        k = s // L
        x = u[row:row + b * s].reshape(b, k, L, groups, C)
        x = jnp.transpose(x, (3, 1, 0, 2, 4)).reshape(groups, k, b, L * C)
        parts.append(jnp.pad(x, ((0, 0), (0, kmax - k), (0, 0), (0, 0))))
        row += b * s
    x = jnp.concatenate(parts, axis=2)
    nseq = x.shape[2]
    return x.reshape(groups, kmax * nseq, L * C), nseq, kmax


def _from_chunks(yv, seqs, nseq, kmax):
    L, C = SSM_CHUNK, SSM_GROUP
    groups = yv.shape[0]
    y = yv.reshape(groups, kmax, nseq, L, C)
    parts, b0 = [], 0
    for b, s in seqs:
        k = s // L
        x = jnp.transpose(y[:, :k, b0:b0 + b], (2, 1, 3, 0, 4))
        parts.append(x.reshape(b * s, groups * C))
        b0 += b
    return jnp.concatenate(parts, axis=0)


def _merge_kernel(x_ref, a_ref, y_ref, ga_ref, gs_ref, wglu_ref, bglu_ref, wao_ref, wso_ref, wout_ref, o_ref):
    z = jax.nn.gelu(y_ref[...].astype(F32))
    z = z * jax.nn.sigmoid(_dot(z.astype(BF16), wglu_ref[...]) + bglu_ref[...])
    m = _dot(z.astype(BF16), wso_ref[...])
    a = _dot(a_ref[...], wao_ref[...])
    merged = jax.nn.sigmoid(ga_ref[...].astype(F32)) * a + jax.nn.sigmoid(gs_ref[...].astype(F32)) * m
    o_ref[...] = x_ref[...] + _dot(merged.astype(BF16), wout_ref[...])


def _merge(x, attn, y, p, col_ga, col_gs, w_glu, b_glu, w_ao, w_so, w_out):
    t, d = x.shape
    aw, sw = attn.shape[1], y.shape[1]
    tm = MERGE_ROWS
    assert col_ga % d == 0 and col_gs % d == 0
    const = lambda shape: pl.BlockSpec(shape, lambda i: (0, 0), pipeline_mode=pl.Buffered(1))
    weights = (sw * sw + aw * d + sw * d + d * d) * 2
    vmem = weights + 2 * tm * (2 * d * 4 + (aw + sw + 2 * d) * 2) + 6 * tm * d * 4 + (4 << 20)
    return pl.pallas_call(
        _merge_kernel,
        out_shape=jax.ShapeDtypeStruct((t, d), F32),
        grid=(t // tm,),
        in_specs=[
            pl.BlockSpec((tm, d), lambda i: (i, 0)),
            pl.BlockSpec((tm, aw), lambda i: (i, 0)),
            pl.BlockSpec((tm, sw), lambda i: (i, 0)),
            pl.BlockSpec((tm, d), lambda i: (i, col_ga // d)),
            pl.BlockSpec((tm, d), lambda i: (i, col_gs // d)),
            const((sw, sw)), const((1, sw)), const((aw, d)), const((sw, d)), const((d, d)),
        ],
        out_specs=pl.BlockSpec((tm, d), lambda i: (i, 0)),
        compiler_params=_cparams(("parallel",), vmem),
        name="mixer_out",
    )(x, attn, y, p, p, w_glu, b_glu.reshape(1, sw), w_ao, w_so, w_out)


def _layer(xs, ffn1_norm, ffn1_w_gate_up, ffn1_w_down, mix_norm, w_in, q_norm, k_norm, attn_sink,
           ssm_lambda_re, ssm_lambda_im, ssm_log_dt, ssm_b_re, ssm_b_im, ssm_c_re, ssm_c_im, ssm_d,
           w_glu, b_glu, w_attn_out, w_ssm_out, w_out, ffn2_norm, ffn2_w_gate_up, ffn2_w_down):
    d = xs[0].shape[-1]
    seqs = [(x.shape[0], x.shape[1]) for x in xs]
    x = jnp.concatenate([x.reshape(-1, d) for x in xs], axis=0)
    aw = N_Q_HEADS * HEAD_DIM
    kw = N_KV_HEADS * HEAD_DIM
    sw = ssm_d.shape[0]

    x = _ffn(x, ffn1_norm, ffn1_w_gate_up.astype(BF16), ffn1_w_down.astype(BF16))

    o_k, o_v, o_u, o_ga, o_gs = aw, aw + kw, aw + 2 * kw, aw + 2 * kw + sw, aw + 2 * kw + sw + d
    w_perm = jnp.concatenate([w_in[:, :o_k], w_in[:, o_u:o_ga], w_in[:, o_ga:o_gs], w_in[:, o_gs:],
                              w_in[:, o_k:o_v], w_in[:, o_v:o_u]], axis=1).astype(BF16)
    col_q, col_u, col_ga, col_gs = 0, aw, aw + sw, aw + sw + d
    col_k, col_v = col_gs + d, col_gs + d + kw
    p = _inproj(x, mix_norm, w_perm)

    attn_parts, row0 = [], 0
    for b, s in seqs:
        cos2, sin2 = _rope_tables(s)
        attn_parts.append(_attention(p, row0, b, s, col_q, col_k, col_v, cos2, sin2, q_norm, k_norm, attn_sink))
        row0 += b * s
    attn = jnp.concatenate(attn_parts, axis=0)

    t_mat, wz, wy, dec = _ssm_prep(ssm_lambda_re, ssm_lambda_im, ssm_log_dt, ssm_b_re, ssm_b_im,
                                   ssm_c_re, ssm_c_im, ssm_d)
    uv, nseq, kmax = _to_chunks(p[:, col_u:col_u + sw], seqs)
    yv = _ssm(uv, t_mat, wz, wy, dec, nseq, kmax)
    y = _from_chunks(yv, seqs, nseq, kmax)

    x = _merge(x, attn, y, p, col_ga, col_gs, w_glu.astype(BF16), b_glu, w_attn_out.astype(BF16),
               w_ssm_out.astype(BF16), w_out.astype(BF16))
    x = _ffn(x, ffn2_norm, ffn2_w_gate_up.astype(BF16), ffn2_w_down.astype(BF16))

    outs, row0 = [], 0
    for (b, s), x_in in zip(seqs, xs):
        outs.append(x[row0:row0 + b * s].reshape(x_in.shape))
        row0 += b * s
    return tuple(outs)


def kernel(x_prompt, x_sample, ffn1_norm, ffn1_w_gate_up, ffn1_w_down, mix_norm, w_in, q_norm, k_norm, attn_sink, ssm_lambda_re, ssm_lambda_im, ssm_log_dt, ssm_b_re, ssm_b_im, ssm_c_re, ssm_c_im, ssm_d, w_glu, b_glu, w_attn_out, w_ssm_out, w_out, ffn2_norm, ffn2_w_gate_up, ffn2_w_down):
    xs = (x_prompt, x_sample)
    depth = ffn1_norm.shape[0]
    params = (ffn1_norm, ffn1_w_gate_up, ffn1_w_down, mix_norm, w_in, q_norm, k_norm, attn_sink,
              ssm_lambda_re, ssm_lambda_im, ssm_log_dt, ssm_b_re, ssm_b_im, ssm_c_re, ssm_c_im, ssm_d,
              w_glu, b_glu, w_attn_out, w_ssm_out, w_out, ffn2_norm, ffn2_w_gate_up, ffn2_w_down)
    for l in range(depth):
        xs = _layer(xs, *(p[l] for p in params))
    return xs
```

```python
import functools
import math

import jax
import jax.numpy as jnp
from jax import lax
from jax.experimental import pallas as pl
from jax.experimental.pallas import tpu as pltpu

F32 = jnp.float32
BF16 = jnp.bfloat16

HEAD_DIM = 128
N_Q_HEADS = 8
N_KV_HEADS = 2
Q_PER_KV = N_Q_HEADS // N_KV_HEADS
WINDOW = 128
ATTN_BLOCK = 128
ROPE_THETA = 10000.0
SSM_GROUP = 16
SSM_STATE = 64
EPS = 1e-6

LANES = 128
VMEM_LIMIT_CAP = 56 * 1024 * 1024

SSM_CHUNK = 64
CHUNK_W = SSM_CHUNK * SSM_GROUP
TOK_BLOCK = 128
FFN_ROWS = 512
FFN_COLS = 512
INPROJ_ROWS = 512
MERGE_ROWS = 256

assert 2 * SSM_CHUNK == TOK_BLOCK == LANES and 2 * SSM_STATE == LANES


def _cparams(semantics, vmem_bytes):
    return pltpu.CompilerParams(dimension_semantics=semantics,
                                vmem_limit_bytes=min(int(vmem_bytes), VMEM_LIMIT_CAP))


def _rms(x, gain):
    return x * lax.rsqrt(jnp.mean(x * x, axis=-1, keepdims=True) + EPS) * gain


def _dot(a, b):
    return jnp.dot(a, b, preferred_element_type=F32)


def _dot_exact(a, b):
    return jnp.dot(a, b, preferred_element_type=F32, precision=lax.Precision.HIGHEST)


def _ffn_kernel(x_ref, gain_ref, wg_ref, wu_ref, wd_ref, o_ref, h_ref):
    f = pl.program_id(1)

    @pl.when(f == 0)
    def _():
        h_ref[...] = _rms(x_ref[...], gain_ref[...]).astype(BF16)
        o_ref[...] = jnp.zeros_like(o_ref)

    h = h_ref[...]
    g = _dot(h, wg_ref[...])
    u = _dot(h, wu_ref[...])
    a = (g * jax.nn.sigmoid(g)) * u
    o_ref[...] += _dot(a.astype(BF16), wd_ref[...])

    @pl.when(f == pl.num_programs(1) - 1)
    def _():
        o_ref[...] = x_ref[...] + 0.5 * o_ref[...]


def _ffn(x, gain, w_gate_up, w_down):
    t, d = x.shape
    d_ff = w_down.shape[0]
    tm, tf = FFN_ROWS, FFN_COLS
    nf = d_ff // tf
    vmem = (4 * tm * d * 4) + (tm * d * 2) + 2 * 3 * (d * tf * 2) + 4 * (tm * tf * 4) + (4 << 20)
    return pl.pallas_call(
        _ffn_kernel,
        out_shape=jax.ShapeDtypeStruct((t, d), F32),
        grid=(t // tm, nf),
        in_specs=[
            pl.BlockSpec((tm, d), lambda i, f: (i, 0)),
            pl.BlockSpec((1, d), lambda i, f: (0, 0)),
            pl.BlockSpec((d, tf), lambda i, f: (0, f)),
            pl.BlockSpec((d, tf), lambda i, f: (0, f + nf)),
            pl.BlockSpec((tf, d), lambda i, f: (f, 0)),
        ],
        out_specs=pl.BlockSpec((tm, d), lambda i, f: (i, 0)),
        scratch_shapes=[pltpu.VMEM((tm, d), BF16)],
        compiler_params=_cparams(("parallel", "arbitrary"), vmem),
        name="ffn",
    )(x, gain.reshape(1, d), w_gate_up, w_gate_up, w_down)


def _inproj_gates_kernel(x_ref, gain_ref, w_ref, o_ref):
    h = _rms(x_ref[...], gain_ref[...]).astype(BF16)
    o_ref[...] = _dot(h, w_ref[...]).astype(BF16)


def _rope(x, cos2, sin2):
    return x * cos2 + pltpu.roll(x, HEAD_DIM // 2, 1) * sin2


def _inproj_qkv_kernel(pos_ref, x_ref, gain_ref, w_ref, cos_ref, sin_ref, qg_ref, kg_ref, o_ref):
    del pos_ref
    h = _rms(x_ref[...], gain_ref[...]).astype(BF16)
    r = _dot(h, w_ref[...])
    cos2, sin2 = cos_ref[...], sin_ref[...]
    qg = qg_ref[...] * (HEAD_DIM ** -0.5)
    kg = kg_ref[...]
    for head in range(N_Q_HEADS + N_KV_HEADS):
        cols = slice(head * HEAD_DIM, (head + 1) * HEAD_DIM)
        gain = qg if head < N_Q_HEADS else kg
        o_ref[:, cols] = _rope(_rms(r[:, cols], gain), cos2, sin2).astype(BF16)
    vcols = slice((N_Q_HEADS + N_KV_HEADS) * HEAD_DIM, (N_Q_HEADS + 2 * N_KV_HEADS) * HEAD_DIM)
    o_ref[:, vcols] = r[:, vcols].astype(BF16)


def _inproj_u_kernel(x_ref, gain_ref, wt_ref, o_ref):
    h = _rms(x_ref[...], gain_ref[...]).astype(BF16)
    rt = lax.dot_general(wt_ref[...], h, (((1,), (1,)), ((), ())), preferred_element_type=F32)
    for j in range(o_ref.shape[0]):
        o_ref[j] = rt[:, j * TOK_BLOCK:(j + 1) * TOK_BLOCK].astype(BF16)


def _const_spec(shape):
    return pl.BlockSpec(shape, lambda *_: (0,) * len(shape), pipeline_mode=pl.Buffered(1))


def _inproj_gates(x, gain, w, tm):
    t, d = x.shape
    n = w.shape[1]
    tn = n // 2
    vmem = 2 * (tm * d * 4) + 2 * (d * tn * 2) + 2 * (tm * tn * 2) + (tm * tn * 4) + (tm * d * 6) + (4 << 20)
    return pl.pallas_call(
        _inproj_gates_kernel,
        out_shape=jax.ShapeDtypeStruct((t, n), BF16),
        grid=(2, t // tm),
        in_specs=[
            pl.BlockSpec((tm, d), lambda j, i: (i, 0)),
            pl.BlockSpec((1, d), lambda j, i: (0, 0)),
            pl.BlockSpec((d, tn), lambda j, i: (0, j)),
        ],
        out_specs=pl.BlockSpec((tm, tn), lambda j, i: (i, j)),
        compiler_params=_cparams(("parallel", "parallel"), vmem),
        name="inproj_gates",
    )(x, gain.reshape(1, d), w)


def _inproj_qkv(x, gain, w, tm, pos_blk, cos2, sin2, q_gain, k_gain):
    t, d = x.shape
    n = w.shape[1]
    vmem = 2 * (tm * d * 4) + (d * n * 2) + 2 * (tm * n * 2) + 2 * (tm * n * 4) + (tm * d * 6) + (6 << 20)
    grid_spec = pltpu.PrefetchScalarGridSpec(
        num_scalar_prefetch=1,
        grid=(t // tm,),
        in_specs=[
            pl.BlockSpec((tm, d), lambda i, pos: (i, 0)),
            _const_spec((1, d)),
            _const_spec((d, n)),
            pl.BlockSpec((tm, HEAD_DIM), lambda i, pos: (pos[i], 0)),
            pl.BlockSpec((tm, HEAD_DIM), lambda i, pos: (pos[i], 0)),
            _const_spec((1, HEAD_DIM)),
            _const_spec((1, HEAD_DIM)),
        ],
        out_specs=pl.BlockSpec((tm, n), lambda i, pos: (i, 0)),
    )
    return pl.pallas_call(
        _inproj_qkv_kernel,
        out_shape=jax.ShapeDtypeStruct((t, n), BF16),
        grid_spec=grid_spec,
        compiler_params=_cparams(("parallel",), vmem),
        name="inproj_qkv",
    )(pos_blk, x, gain.reshape(1, d), w, cos2, sin2, q_gain.reshape(1, HEAD_DIM), k_gain.reshape(1, HEAD_DIM))


def _inproj_u(x, gain, wt, tm):
    t, d = x.shape
    n = wt.shape[0]
    vmem = 2 * (tm * d * 4) + (d * n * 2) + 2 * (tm * n * 2) + 2 * (tm * n * 4) + (tm * d * 6) + (6 << 20)
    return pl.pallas_call(
        _inproj_u_kernel,
        out_shape=jax.ShapeDtypeStruct((t // TOK_BLOCK, n, TOK_BLOCK), BF16),
        grid=(t // tm,),
        in_specs=[
            pl.BlockSpec((tm, d), lambda i: (i, 0)),
            _const_spec((1, d)),
            _const_spec((n, d)),
        ],
        out_specs=pl.BlockSpec((tm // TOK_BLOCK, n, TOK_BLOCK), lambda i: (i, 0, 0)),
        compiler_params=_cparams(("parallel",), vmem),
        name="inproj_u",
    )(x, gain.reshape(1, d), wt)


def _rope_table_kernel(freq_ref, cos_ref, sin_ref):
    rows = cos_ref.shape[0]
    pos = (lax.broadcasted_iota(jnp.int32, (rows, HEAD_DIM), 0) + pl.program_id(0) * rows).astype(F32)
    ang = pos * freq_ref[...]
    lane = lax.broadcasted_iota(jnp.int32, (rows, HEAD_DIM), 1)
    cos_ref[...] = jnp.cos(ang)
    sin_ref[...] = jnp.where(lane < HEAD_DIM // 2, -1.0, 1.0) * jnp.sin(ang)


def _rope_tables(seq):
    inv_freq = ROPE_THETA ** (-jnp.arange(0, HEAD_DIM, 2, dtype=F32) / HEAD_DIM)
    freq2 = jnp.concatenate([inv_freq, inv_freq]).reshape(1, HEAD_DIM)
    rows = math.gcd(seq, 512)
    return pl.pallas_call(
        _rope_table_kernel,
        out_shape=(jax.ShapeDtypeStruct((seq, HEAD_DIM), F32),) * 2,
        grid=(seq // rows,),
        in_specs=[pl.BlockSpec((1, HEAD_DIM), lambda i: (0, 0))],
        out_specs=(pl.BlockSpec((rows, HEAD_DIM), lambda i: (i, 0)),) * 2,
        compiler_params=_cparams(("parallel",), 16 << 20),
        name="rope_tables",
    )(freq2)


def _attn_kernel(sink_ref, q_ref, kp_ref, kc_ref, kn_ref, vp_ref, vc_ref, vn_ref, o_ref):
    n = pl.program_id(1)
    nb = pl.num_programs(1)
    blk = ATTN_BLOCK
    nq = Q_PER_KV * blk
    kj = lax.broadcasted_iota(jnp.int32, (3 * blk, nq), 0)
    qi = lax.broadcasted_iota(jnp.int32, (3 * blk, nq), 1) & (blk - 1)
    rel = kj - blk - qi
    ok = (rel >= -WINDOW) & (rel <= WINDOW)
    ok = ok & ((kj >= blk) | (n > 0)) & ((kj < 2 * blk) | (n < nb - 1))

    for h in range(N_KV_HEADS):
        cols = slice(h * HEAD_DIM, (h + 1) * HEAD_DIM)
        k_band = jnp.concatenate([kp_ref[:, cols], kc_ref[:, cols], kn_ref[:, cols]], axis=0)
        v_band = jnp.concatenate([vp_ref[:, cols], vc_ref[:, cols], vn_ref[:, cols]], axis=0)
        heads = range(h * Q_PER_KV, (h + 1) * Q_PER_KV)
        q_all = jnp.concatenate([q_ref[:, hd * HEAD_DIM:(hd + 1) * HEAD_DIM] for hd in heads], axis=0)
        sink = jnp.concatenate([jnp.full((1, blk), sink_ref[hd], F32) for hd in heads], axis=1)
        s = lax.dot_general(k_band, q_all, (((1,), (1,)), ((), ())), preferred_element_type=F32)
        s = jnp.where(ok, s, -jnp.inf)
        m = jnp.maximum(jnp.max(s, axis=0, keepdims=True), sink)
        p = jnp.exp(s - m)
        denom = jnp.sum(p, axis=0, keepdims=True) + jnp.exp(sink - m)
        ot = lax.dot_general(v_band, p.astype(BF16), (((0,), (0,)), ((), ())), preferred_element_type=F32)
        ot = ot / denom
        for g, hd in enumerate(heads):
            o_ref[:, hd * HEAD_DIM:(hd + 1) * HEAD_DIM] = ot[:, g * blk:(g + 1) * blk].T.astype(BF16)


def _attention(qkv, row0, batch, seq, sink):
    blk = ATTN_BLOCK
    nb = seq // blk
    base = row0 // blk
    qw = N_Q_HEADS * HEAD_DIM
    kw = N_KV_HEADS * HEAD_DIM
    assert row0 % blk == 0 and qw % kw == 0

    def kv_spec(col_blk, shift):
        return pl.BlockSpec((blk, kw), lambda b, n, s: (base + b * nb + shift(n), col_blk))

    prev = lambda n: jnp.maximum(n - 1, 0)
    ident = lambda n: n
    nxt = lambda n: jnp.minimum(n + 1, nb - 1)
    k_blk, v_blk = qw // kw, qw // kw + 1
    grid_spec = pltpu.PrefetchScalarGridSpec(
        num_scalar_prefetch=1,
        grid=(batch, nb),
        in_specs=[
            pl.BlockSpec((blk, qw), lambda b, n, s: (base + b * nb + n, 0)),
            kv_spec(k_blk, prev), kv_spec(k_blk, ident), kv_spec(k_blk, nxt),
            kv_spec(v_blk, prev), kv_spec(v_blk, ident), kv_spec(v_blk, nxt),
        ],
        out_specs=pl.BlockSpec((blk, qw), lambda b, n, s: (b * nb + n, 0)),
    )
    return pl.pallas_call(
        _attn_kernel,
        out_shape=jax.ShapeDtypeStruct((batch * seq, qw), BF16),
        grid_spec=grid_spec,
        compiler_params=_cparams(("parallel", "parallel"), 32 << 20),
        name="attention",
    )(sink, qkv, qkv, qkv, qkv, qkv, qkv, qkv)


def _cmul(ar, ai, br, bi):
    return ar * br - ai * bi, ar * bi + ai * br


def _cpow_table(br, bi, expo, nbits):
    pr = jnp.ones(expo.shape, F32)
    pi = jnp.zeros(expo.shape, F32)
    br = jnp.broadcast_to(br, expo.shape)
    bi = jnp.broadcast_to(bi, expo.shape)
    for k in range(nbits):
        bit = ((expo >> k) & 1) == 1
        nr, ni = _cmul(pr, pi, br, bi)
        pr = jnp.where(bit, nr, pr)
        pi = jnp.where(bit, ni, pi)
        if k + 1 < nbits:
            br, bi = _cmul(br, bi, br, bi)
    return pr, pi


def _lambda_bar(lre, lim, log_dt):
    dt = jnp.exp(log_dt)
    mag = jnp.exp(lre * dt)
    return mag * jnp.cos(lim * dt), mag * jnp.sin(lim * dt)


def _ssm_weights(rowp_ref, colp_ref, btr_ref, bti_ref, crr_ref, cri_ref, ctr_ref, cti_ref, dcol_ref,
                 t_s, wz_s, wy_s, g_s):
    L, P, C = SSM_CHUNK, SSM_STATE, SSM_GROUP
    nbits = L.bit_length()

    lre, lim, ldt = rowp_ref[0, 0:1, :], rowp_ref[0, 1:2, :], rowp_ref[0, 2:3, :]
    lbr, lbi = _lambda_bar(lre, lim, ldt)
    den = lre * lre + lim * lim
    cfr = ((lbr - 1.0) * lre + lbi * lim) / den
    cfi = (lbi * lre - (lbr - 1.0) * lim) / den
    bbr, bbi = _cmul(cfr, cfi, btr_ref[0], bti_ref[0])

    dr, di = lbr, lbi
    for _ in range(L.bit_length() - 1):
        dr, di = _cmul(dr, di, dr, di)

    n_sub = lax.broadcasted_iota(jnp.int32, (L, 2 * P), 0)
    lane = lax.broadcasted_iota(jnp.int32, (L, 2 * P), 1)
    pzr, pzi = _cpow_table(lbr, lbi, jnp.where(lane < P, L - 1 - n_sub, n_sub), nbits)
    for c in range(C):
        zr, zi = _cmul(pzr, pzi, bbr[c:c + 1], bbi[c:c + 1])
        wz_s[c * L:(c + 1) * L, 0:2 * P] = zr.astype(BF16)
        wz_s[c * L:(c + 1) * L, 2 * P:4 * P] = zi.astype(BF16)

    b_rep_r = jnp.concatenate([jnp.broadcast_to(bbr[c:c + 1], (C, 2 * P)) for c in range(C)], axis=0)
    b_rep_i = jnp.concatenate([jnp.broadcast_to(bbi[c:c + 1], (C, 2 * P)) for c in range(C)], axis=0)
    c_tile_r = jnp.concatenate([crr_ref[0]] * C, axis=0)
    c_tile_i = jnp.concatenate([cri_ref[0]] * C, axis=0)
    bcr, bci = _cmul(b_rep_r, b_rep_i, c_tile_r, c_tile_i)

    clre, clim, cldt = colp_ref[0, :, 0:1], colp_ref[0, :, 1:2], colp_ref[0, :, 2:3]
    cbr, cbi = _lambda_bar(clre, clim, cldt)
    sub = lax.broadcasted_iota(jnp.int32, (2 * P, LANES), 0)
    m_lane = lax.broadcasted_iota(jnp.int32, (2 * P, LANES), 1)

    pgr, pgi = _cpow_table(cbr, cbi, jnp.abs(m_lane - (L - 1)), nbits)
    live = ((sub < P) & (m_lane >= L - 1)) | ((sub >= P) & (m_lane <= L - 1))
    pgr = jnp.where(live, pgr, 0.0)
    pgi = jnp.where(live, pgi, 0.0)
    g = _dot_exact(bcr, pgr) - _dot_exact(bci, pgi)
    g_row = lax.broadcasted_iota(jnp.int32, (C * C, LANES), 0)
    g_lane = lax.broadcasted_iota(jnp.int32, (C * C, LANES), 1)
    log2c = C.bit_length() - 1
    skip = ((g_row >> log2c) == (g_row & (C - 1))) & (g_lane == L - 1)
    g_s[...] = g + jnp.where(skip, dcol_ref[0], 0.0)

    j_lane = m_lane & (L - 1)
    pyr, pyi = _cpow_table(cbr, cbi, jnp.where(sub < P, j_lane + 1, L - j_lane), nbits)
    ctr, cti = ctr_ref[0], cti_ref[0]
    low = m_lane < L
    for a in range(C // 2):
        cr = jnp.where(low, ctr[:, 2 * a:2 * a + 1], ctr[:, 2 * a + 1:2 * a + 2])
        ci = jnp.where(low, cti[:, 2 * a:2 * a + 1], cti[:, 2 * a + 1:2 * a + 2])
        yr, yi = _cmul(cr, ci, pyr, pyi)
        wy_s[0:2 * P, a * LANES:(a + 1) * LANES] = yr.astype(BF16)
        wy_s[2 * P:4 * P, a * LANES:(a + 1) * LANES] = (-yi).astype(BF16)

    t_low = lax.broadcasted_iota(jnp.int32, (L, LANES), 1) < L

    def fill(c_in, carry):
        for a in range(C // 2):
            ge = jnp.broadcast_to(g_s[pl.ds(c_in * C + 2 * a, 1), :], (L, LANES))
            go = jnp.broadcast_to(g_s[pl.ds(c_in * C + 2 * a + 1, 1), :], (L, LANES))
            te = pltpu.roll(ge, L + 1, 1, stride=1, stride_axis=0)
            to = pltpu.roll(go, 1, 1, stride=1, stride_axis=0)
            t_s[pl.ds(pl.multiple_of(c_in * L, L), L), a * LANES:(a + 1) * LANES] = (
                jnp.where(t_low, te, to).astype(BF16))
        return carry

    lax.fori_loop(0, C, fill, 0)
    return dr, di


def _ssm_kernel(u_ref, rowp_ref, colp_ref, btr_ref, bti_ref, crr_ref, cri_ref, ctr_ref, cti_ref, dcol_ref,
                y_ref, t_s, wz_s, wy_s, g_s, a_s, yt_s, zr_s, zi_s, fr_s, fi_s, rr_s, ri_s, *, seqs):
    P, L, C = SSM_STATE, SSM_CHUNK, SSM_GROUP
    nblk = u_ref.shape[0]
    ar, ai = _ssm_weights(rowp_ref, colp_ref, btr_ref, bti_ref, crr_ref, cri_ref, ctr_ref, cti_ref, dcol_ref,
                          t_s, wz_s, wy_s, g_s)

    a_s[...] = u_ref[...].astype(F32)
    low = lax.broadcasted_iota(jnp.int32, (nblk, LANES), 1) < L
    first, second = [], []
    for a in range(C // 2):
        x0, x1 = a_s[:, 2 * a, :], a_s[:, 2 * a + 1, :]
        first.append(jnp.where(low, x0, pltpu.roll(x1, L, 1)))
        second.append(jnp.where(low, pltpu.roll(x0, L, 1), x1))
    u = jnp.concatenate([jnp.concatenate(first, axis=1), jnp.concatenate(second, axis=1)], axis=0).astype(BF16)

    z = _dot(u, wz_s[...])
    zr_s[...] = z[:, 0:2 * P]
    zi_s[...] = z[:, 2 * P:4 * P]

    def scan(blk0, nseq, n):
        is_fwd = lax.broadcasted_iota(jnp.int32, (nseq, 2 * P), 1) < P

        def rows(block, second_half):
            return pl.ds(blk0 + block + (nblk if second_half else 0), nseq, stride=n)

        def half(carry, f_rows, b_rows):
            sr, si = carry
            fr_s[f_rows, :] = sr
            fi_s[f_rows, :] = si
            rr_s[b_rows, :] = sr
            ri_s[b_rows, :] = si
            zr = jnp.where(is_fwd, zr_s[f_rows, :], zr_s[b_rows, :])
            zi = jnp.where(is_fwd, zi_s[f_rows, :], zi_s[b_rows, :])
            return ar * sr - ai * si + zr, ar * si + ai * sr + zi

        def step(t, carry):
            carry = half(carry, rows(t, False), rows(n - 1 - t, True))
            return half(carry, rows(t, True), rows(n - 1 - t, False))

        zero = jnp.zeros((nseq, 2 * P), F32)
        lax.fori_loop(0, n, step, (zero, zero))

    blk0 = 0
    for nseq, slen in seqs:
        scan(blk0, nseq, slen // TOK_BLOCK)
        blk0 += nseq * (slen // TOK_BLOCK)

    sel = lax.broadcasted_iota(jnp.int32, fr_s.shape, 1) < P
    s_in = jnp.concatenate([jnp.where(sel, fr_s[...], rr_s[...]),
                            jnp.where(sel, fi_s[...], ri_s[...])], axis=1).astype(BF16)
    y = _dot(u, t_s[...]) + _dot(s_in, wy_s[...])

    for a in range(C // 2):
        y0 = y[0:nblk, a * LANES:(a + 1) * LANES]
        y1 = y[nblk:2 * nblk, a * LANES:(a + 1) * LANES]
        yt_s[:, 2 * a, :] = jnp.where(low, y0, pltpu.roll(y1, L, 1))
        yt_s[:, 2 * a + 1, :] = jnp.where(low, pltpu.roll(y0, L, 1), y1)
    y_ref[...] = yt_s[...].astype(y_ref.dtype)


def _ssm(ut, seqs, lam_re, lam_im, log_dt, b_re, b_im, c_re, c_im, d_skip):
    nblk, width, _ = ut.shape
    _, groups, P = lam_re.shape
    C = SSM_GROUP
    assert P == SSM_STATE and b_re.shape[-1] == C and groups * C == width
    assert all(s % TOK_BLOCK == 0 for _, s in seqs)

    def lanes_fb(a):
        return jnp.transpose(a, (1, 0, 2)).reshape(groups, 2 * P)

    ldt = jnp.broadcast_to(log_dt[:, :, None], (2, groups, P))
    rowp = jnp.stack([lanes_fb(lam_re), lanes_fb(lam_im), lanes_fb(ldt)], axis=1)
    colp = jnp.transpose(rowp, (0, 2, 1))
    bt = lambda b: jnp.transpose(b, (1, 3, 0, 2)).reshape(groups, C, 2 * P)
    cr = lambda c: jnp.transpose(c, (1, 2, 0, 3)).reshape(groups, C, 2 * P)
    ct = lambda c: jnp.transpose(c, (1, 0, 3, 2)).reshape(groups, 2 * P, C)
    dcol = jnp.tile(d_skip.reshape(groups, 1, C), (1, C, 1)).reshape(groups, C * C, 1)

    g3 = lambda s1, s2: pl.BlockSpec((1, s1, s2), lambda g: (g, 0, 0))
    act = pl.BlockSpec((nblk, C, TOK_BLOCK), lambda g: (0, g, 0))
    rows2 = 2 * nblk
    return pl.pallas_call(
        functools.partial(_ssm_kernel, seqs=tuple(seqs)),
        out_shape=jax.ShapeDtypeStruct(ut.shape, BF16),
        grid=(groups,),
        in_specs=[act, g3(3, 2 * P), g3(2 * P, 3), g3(C, 2 * P), g3(C, 2 * P), g3(C, 2 * P), g3(C, 2 * P),
                  g3(2 * P, C), g3(2 * P, C), g3(C * C, 1)],
        out_specs=act,
        scratch_shapes=[
            pltpu.VMEM((CHUNK_W, CHUNK_W), BF16),
            pltpu.VMEM((CHUNK_W, 4 * P), BF16),
            pltpu.VMEM((4 * P, CHUNK_W), BF16),
            pltpu.VMEM((C * C, LANES), F32),
            pltpu.VMEM((nblk, C, TOK_BLOCK), F32),
            pltpu.VMEM((nblk, C, TOK_BLOCK), F32),
        ] + [pltpu.VMEM((rows2, 2 * P), F32)] * 6,
        compiler_params=_cparams(("parallel",), 40 << 20),
        name="ssm",
    )(ut, rowp, colp, bt(b_re), bt(b_im), cr(c_re), cr(c_im), ct(c_re), ct(c_im), dcol)


def _merge_kernel(x_ref, a_ref, yt_ref, ga_ref, gs_ref, wglu_ref, bglu_ref, wao_ref, wso_ref, wout_ref, o_ref):
    y = jnp.concatenate([yt_ref[j].astype(F32).T for j in range(yt_ref.shape[0])], axis=0)
    z = jax.nn.gelu(y)
    z = z * jax.nn.sigmoid(_dot(z.astype(BF16), wglu_ref[...]) + bglu_ref[...])
    m = _dot(z.astype(BF16), wso_ref[...])
    a = _dot(a_ref[...], wao_ref[...])
    merged = jax.nn.sigmoid(ga_ref[...].astype(F32)) * a + jax.nn.sigmoid(gs_ref[...].astype(F32)) * m
    o_ref[...] = x_ref[...] + _dot(merged.astype(BF16), wout_ref[...])


def _merge(x, attn, yt, gates, w_glu, b_glu, w_ao, w_so, w_out):
    t, d = x.shape
    aw, sw = attn.shape[1], yt.shape[1]
    tm = MERGE_ROWS
    weights = (sw * sw + aw * d + sw * d + d * d) * 2
    vmem = weights + 2 * tm * (2 * d * 4 + (aw + sw + 2 * d) * 2) + 6 * tm * d * 4 + (4 << 20)
    return pl.pallas_call(
        _merge_kernel,
        out_shape=jax.ShapeDtypeStruct((t, d), F32),
        grid=(t // tm,),
        in_specs=[
            pl.BlockSpec((tm, d), lambda i: (i, 0)),
            pl.BlockSpec((tm, aw), lambda i: (i, 0)),
            pl.BlockSpec((tm // TOK_BLOCK, sw, TOK_BLOCK), lambda i: (i, 0, 0)),
            pl.BlockSpec((tm, d), lambda i: (i, 0)),
            pl.BlockSpec((tm, d), lambda i: (i, 1)),
            _const_spec((sw, sw)), _const_spec((1, sw)), _const_spec((aw, d)), _const_spec((sw, d)),
            _const_spec((d, d)),
        ],
        out_specs=pl.BlockSpec((tm, d), lambda i: (i, 0)),
        compiler_params=_cparams(("parallel",), vmem),
        name="mixer_out",
    )(x, attn, yt, gates, gates, w_glu, b_glu.reshape(1, sw), w_ao, w_so, w_out)


def _layer(xs, ffn1_norm, ffn1_w_gate_up, ffn1_w_down, mix_norm, w_in, q_norm, k_norm, attn_sink,
           ssm_lambda_re, ssm_lambda_im, ssm_log_dt, ssm_b_re, ssm_b_im, ssm_c_re, ssm_c_im, ssm_d,
           w_glu, b_glu, w_attn_out, w_ssm_out, w_out, ffn2_norm, ffn2_w_gate_up, ffn2_w_down):
    d = xs[0].shape[-1]
    seqs = [(x.shape[0], x.shape[1]) for x in xs]
    x = jnp.concatenate([x.reshape(-1, d) for x in xs], axis=0)
    aw = N_Q_HEADS * HEAD_DIM
    kw = N_KV_HEADS * HEAD_DIM
    sw = ssm_d.shape[0]

    x = _ffn(x, ffn1_norm, ffn1_w_gate_up.astype(BF16), ffn1_w_down.astype(BF16))

    o_u, o_g = aw + 2 * kw, aw + 2 * kw + sw
    tm = math.gcd(INPROJ_ROWS, *(s for _, s in seqs))
    max_seq = max(s for _, s in seqs)
    cos2, sin2 = _rope_tables(max_seq)
    pos_blk = jnp.concatenate([jnp.tile(jnp.arange(s // tm, dtype=jnp.int32), b) for b, s in seqs])
    qkv = _inproj_qkv(x, mix_norm, w_in[:, :o_u].astype(BF16), tm, pos_blk, cos2, sin2, q_norm, k_norm)
    ut = _inproj_u(x, mix_norm, w_in[:, o_u:o_g].T.astype(BF16), tm)
    gates = _inproj_gates(x, mix_norm, w_in[:, o_g:].astype(BF16), tm)

    attn_parts, row0 = [], 0
    for b, s in seqs:
        attn_parts.append(_attention(qkv, row0, b, s, attn_sink))
        row0 += b * s
    attn = jnp.concatenate(attn_parts, axis=0)

    yt = _ssm(ut, seqs, ssm_lambda_re, ssm_lambda_im, ssm_log_dt, ssm_b_re, ssm_b_im, ssm_c_re, ssm_c_im, ssm_d)

    x = _merge(x, attn, yt, gates, w_glu.astype(BF16), b_glu, w_attn_out.astype(BF16),
               w_ssm_out.astype(BF16), w_out.astype(BF16))
    x = _ffn(x, ffn2_norm, ffn2_w_gate_up.astype(BF16), ffn2_w_down.astype(BF16))

    outs, row0 = [], 0
    for (b, s), x_in in zip(seqs, xs):
        outs.append(x[row0:row0 + b * s].reshape(x_in.shape))
        row0 += b * s
    return tuple(outs)


def kernel(x_prompt, x_sample, ffn1_norm, ffn1_w_gate_up, ffn1_w_down, mix_norm, w_in, q_norm, k_norm, attn_sink, ssm_lambda_re, ssm_lambda_im, ssm_log_dt, ssm_b_re, ssm_b_im, ssm_c_re, ssm_c_im, ssm_d, w_glu, b_glu, w_attn_out, w_ssm_out, w_out, ffn2_norm, ffn2_w_gate_up, ffn2_w_down):
    xs = (x_prompt, x_sample)
    depth = ffn1_norm.shape[0]
    params = (ffn1_norm, ffn1_w_gate_up, ffn1_w_down, mix_norm, w_in, q_norm, k_norm, attn_sink,
              ssm_lambda_re, ssm_lambda_im, ssm_log_dt, ssm_b_re, ssm_b_im, ssm_c_re, ssm_c_im, ssm_d,
              w_glu, b_glu, w_attn_out, w_ssm_out, w_out, ffn2_norm, ffn2_w_gate_up, ffn2_w_down)
    for l in range(depth):
        xs = _layer(xs, *(p[l] for p in params))
    return xs
```

```python
import functools
import math

import jax
import jax.numpy as jnp
from jax import lax
from jax.experimental import pallas as pl
from jax.experimental.pallas import tpu as pltpu

F32 = jnp.float32
BF16 = jnp.bfloat16

HEAD_DIM = 128
N_Q_HEADS = 8
N_KV_HEADS = 2
Q_PER_KV = N_Q_HEADS // N_KV_HEADS
WINDOW = 128
ATTN_BLOCK = 128
ROPE_THETA = 10000.0
SSM_GROUP = 16
SSM_STATE = 64
EPS = 1e-6

LANES = 128
VMEM_LIMIT_CAP = 56 * 1024 * 1024

SSM_CHUNK = 64
CHUNK_W = SSM_CHUNK * SSM_GROUP
TOK_BLOCK = 128
FFN_ROWS = 512
FFN_COLS = 512
INPROJ_ROWS = 512
MERGE_ROWS = 256

assert 2 * SSM_CHUNK == TOK_BLOCK == LANES and 2 * SSM_STATE == LANES


def _cparams(semantics, vmem_bytes):
    return pltpu.CompilerParams(dimension_semantics=semantics,
                                vmem_limit_bytes=min(int(vmem_bytes), VMEM_LIMIT_CAP))


def _rms(x, gain):
    return x * lax.rsqrt(jnp.mean(x * x, axis=-1, keepdims=True) + EPS) * gain


def _dot(a, b):
    return jnp.dot(a, b, preferred_element_type=F32)


def _dot_exact(a, b):
    return jnp.dot(a, b, preferred_element_type=F32, precision=lax.Precision.HIGHEST)


def _ffn_kernel(x_ref, gain_ref, wg_ref, wu_ref, wd_ref, o_ref, h_ref):
    f = pl.program_id(1)

    @pl.when(f == 0)
    def _():
        h_ref[...] = _rms(x_ref[...], gain_ref[...]).astype(BF16)
        o_ref[...] = jnp.zeros_like(o_ref)

    h = h_ref[...]
    g = _dot(h, wg_ref[...])
    u = _dot(h, wu_ref[...])
    a = (g * jax.nn.sigmoid(g)) * u
    o_ref[...] += _dot(a.astype(BF16), wd_ref[...])

    @pl.when(f == pl.num_programs(1) - 1)
    def _():
        o_ref[...] = x_ref[...] + 0.5 * o_ref[...]


def _ffn(x, gain, w_gate_up, w_down):
    t, d = x.shape
    d_ff = w_down.shape[0]
    tm, tf = FFN_ROWS, FFN_COLS
    nf = d_ff // tf
    vmem = (4 * tm * d * 4) + (tm * d * 2) + 2 * 3 * (d * tf * 2) + 4 * (tm * tf * 4) + (4 << 20)
    return pl.pallas_call(
        _ffn_kernel,
        out_shape=jax.ShapeDtypeStruct((t, d), F32),
        grid=(t // tm, nf),
        in_specs=[
            pl.BlockSpec((tm, d), lambda i, f: (i, 0)),
            pl.BlockSpec((1, d), lambda i, f: (0, 0)),
            pl.BlockSpec((d, tf), lambda i, f: (0, f)),
            pl.BlockSpec((d, tf), lambda i, f: (0, f + nf)),
            pl.BlockSpec((tf, d), lambda i, f: (f, 0)),
        ],
        out_specs=pl.BlockSpec((tm, d), lambda i, f: (i, 0)),
        scratch_shapes=[pltpu.VMEM((tm, d), BF16)],
        compiler_params=_cparams(("parallel", "arbitrary"), vmem),
        name="ffn",
    )(x, gain.reshape(1, d), w_gate_up, w_gate_up, w_down)


def _inproj_gates_kernel(x_ref, gain_ref, w_ref, o_ref):
    h = _rms(x_ref[...], gain_ref[...]).astype(BF16)
    o_ref[...] = _dot(h, w_ref[...]).astype(BF16)


def _rope(x, cos2, sin2):
    return x * cos2 + pltpu.roll(x, HEAD_DIM // 2, 1) * sin2


def _inproj_qkv_kernel(x_ref, gain_ref, w_ref, cos_ref, sin_ref, qg_ref, kg_ref, o_ref):
    qg = qg_ref[...] * (HEAD_DIM ** -0.5)
    kg = kg_ref[...]
    n_rot = N_Q_HEADS + N_KV_HEADS
    half_rows = x_ref.shape[0] // 2
    for half in range(2):
        rows = slice(half * half_rows, (half + 1) * half_rows)
        h = _rms(x_ref[rows, :], gain_ref[...]).astype(BF16)
        r = _dot(h, w_ref[...])
        cos2, sin2 = cos_ref[rows, :], sin_ref[rows, :]
        for head in range(n_rot):
            cols = slice(head * HEAD_DIM, (head + 1) * HEAD_DIM)
            gain = qg if head < N_Q_HEADS else kg
            o_ref[rows, cols] = _rope(_rms(r[:, cols], gain), cos2, sin2).astype(BF16)
        vcols = slice(n_rot * HEAD_DIM, (n_rot + N_KV_HEADS) * HEAD_DIM)
        o_ref[rows, vcols] = r[:, vcols].astype(BF16)


def _inproj_u_kernel(x_ref, gain_ref, wt_ref, o_ref):
    h = _rms(x_ref[...], gain_ref[...]).astype(BF16)
    rt = lax.dot_general(wt_ref[...], h, (((1,), (1,)), ((), ())), preferred_element_type=F32)
    for j in range(o_ref.shape[0]):
        o_ref[j] = rt[:, j * TOK_BLOCK:(j + 1) * TOK_BLOCK].astype(BF16)


def _const_spec(shape):
    return pl.BlockSpec(shape, lambda *_: (0,) * len(shape), pipeline_mode=pl.Buffered(1))


def _inproj_gates(x, gain, w, tm):
    t, d = x.shape
    n = w.shape[1]
    tn = n // 2
    vmem = 2 * (tm * d * 4) + 2 * (d * tn * 2) + 2 * (tm * tn * 2) + (tm * tn * 4) + (tm * d * 6) + (4 << 20)
    return pl.pallas_call(
        _inproj_gates_kernel,
        out_shape=jax.ShapeDtypeStruct((t, n), BF16),
        grid=(2, t // tm),
        in_specs=[
            pl.BlockSpec((tm, d), lambda j, i: (i, 0)),
            pl.BlockSpec((1, d), lambda j, i: (0, 0)),
            pl.BlockSpec((d, tn), lambda j, i: (0, j)),
        ],
        out_specs=pl.BlockSpec((tm, tn), lambda j, i: (i, j)),
        compiler_params=_cparams(("parallel", "parallel"), vmem),
        name="inproj_gates",
    )(x, gain.reshape(1, d), w)


def _inproj_qkv(x, gain, w, tm, seq, cos2, sin2, q_gain, k_gain):
    t, d = x.shape
    n = w.shape[1]
    tiles_per_seq = seq // tm
    vmem = 2 * (tm * d * 4) + (d * n * 2) + 2 * (tm * n * 2) + 2 * (tm * n * 4) + (tm * d * 6) + (6 << 20)
    return pl.pallas_call(
        _inproj_qkv_kernel,
        out_shape=jax.ShapeDtypeStruct((t, n), BF16),
        grid=(t // tm,),
        in_specs=[
            pl.BlockSpec((tm, d), lambda i: (i, 0)),
            _const_spec((1, d)),
            _const_spec((d, n)),
            pl.BlockSpec((tm, HEAD_DIM), lambda i: (i % tiles_per_seq, 0)),
            pl.BlockSpec((tm, HEAD_DIM), lambda i: (i % tiles_per_seq, 0)),
            _const_spec((1, HEAD_DIM)),
            _const_spec((1, HEAD_DIM)),
        ],
        out_specs=pl.BlockSpec((tm, n), lambda i: (i, 0)),
        compiler_params=_cparams(("parallel",), vmem),
        name="inproj_qkv",
    )(x, gain.reshape(1, d), w, cos2, sin2, q_gain.reshape(1, HEAD_DIM), k_gain.reshape(1, HEAD_DIM))


def _inproj_u(x, gain, wt, tm):
    t, d = x.shape
    n = wt.shape[0]
    vmem = 2 * (tm * d * 4) + (d * n * 2) + 2 * (tm * n * 2) + 2 * (tm * n * 4) + (tm * d * 6) + (6 << 20)
    return pl.pallas_call(
        _inproj_u_kernel,
        out_shape=jax.ShapeDtypeStruct((t // TOK_BLOCK, n, TOK_BLOCK), BF16),
        grid=(t // tm,),
        in_specs=[
            pl.BlockSpec((tm, d), lambda i: (i, 0)),
            _const_spec((1, d)),
            _const_spec((n, d)),
        ],
        out_specs=pl.BlockSpec((tm // TOK_BLOCK, n, TOK_BLOCK), lambda i: (i, 0, 0)),
        compiler_params=_cparams(("parallel",), vmem),
        name="inproj_u",
    )(x, gain.reshape(1, d), wt)


def _rope_table_kernel(freq_ref, cos_ref, sin_ref):
    rows = cos_ref.shape[0]
    pos = (lax.broadcasted_iota(jnp.int32, (rows, HEAD_DIM), 0) + pl.program_id(0) * rows).astype(F32)
    ang = pos * freq_ref[...]
    lane = lax.broadcasted_iota(jnp.int32, (rows, HEAD_DIM), 1)
    cos_ref[...] = jnp.cos(ang)
    sin_ref[...] = jnp.where(lane < HEAD_DIM // 2, -1.0, 1.0) * jnp.sin(ang)


def _rope_tables(seq):
    inv_freq = ROPE_THETA ** (-jnp.arange(0, HEAD_DIM, 2, dtype=F32) / HEAD_DIM)
    freq2 = jnp.concatenate([inv_freq, inv_freq]).reshape(1, HEAD_DIM)
    rows = math.gcd(seq, 512)
    return pl.pallas_call(
        _rope_table_kernel,
        out_shape=(jax.ShapeDtypeStruct((seq, HEAD_DIM), F32),) * 2,
        grid=(seq // rows,),
        in_specs=[pl.BlockSpec((1, HEAD_DIM), lambda i: (0, 0))],
        out_specs=(pl.BlockSpec((rows, HEAD_DIM), lambda i: (i, 0)),) * 2,
        compiler_params=_cparams(("parallel",), 16 << 20),
        name="rope_tables",
    )(freq2)


def _attn_kernel(sink_ref, q_ref, kp_ref, kc_ref, kn_ref, vp_ref, vc_ref, vn_ref, o_ref):
    n = pl.program_id(1)
    nb = pl.num_programs(1)
    blk = ATTN_BLOCK
    nq = Q_PER_KV * blk
    kj = lax.broadcasted_iota(jnp.int32, (3 * blk, nq), 0)
    qi = lax.broadcasted_iota(jnp.int32, (3 * blk, nq), 1) & (blk - 1)
    rel = kj - blk - qi
    ok = (rel >= -WINDOW) & (rel <= WINDOW)
    ok = ok & ((kj >= blk) | (n > 0)) & ((kj < 2 * blk) | (n < nb - 1))

    for h in range(N_KV_HEADS):
        cols = slice(h * HEAD_DIM, (h + 1) * HEAD_DIM)
        k_band = jnp.concatenate([kp_ref[:, cols], kc_ref[:, cols], kn_ref[:, cols]], axis=0)
        v_band = jnp.concatenate([vp_ref[:, cols], vc_ref[:, cols], vn_ref[:, cols]], axis=0)
        heads = range(h * Q_PER_KV, (h + 1) * Q_PER_KV)
        q_all = jnp.concatenate([q_ref[:, hd * HEAD_DIM:(hd + 1) * HEAD_DIM] for hd in heads], axis=0)
        sink = jnp.concatenate([jnp.full((1, blk), sink_ref[hd], F32) for hd in heads], axis=1)
        s = lax.dot_general(k_band, q_all, (((1,), (1,)), ((), ())), preferred_element_type=F32)
        s = jnp.where(ok, s, -jnp.inf)
        m = jnp.maximum(jnp.max(s, axis=0, keepdims=True), sink)
        p = jnp.exp(s - m)
        denom = jnp.sum(p, axis=0, keepdims=True) + jnp.exp(sink - m)
        ot = lax.dot_general(v_band, p.astype(BF16), (((0,), (0,)), ((), ())), preferred_element_type=F32)
        ot = ot / denom
        for g, hd in enumerate(heads):
            o_ref[:, hd * HEAD_DIM:(hd + 1) * HEAD_DIM] = ot[:, g * blk:(g + 1) * blk].T.astype(BF16)


def _attention(qkv, batch, seq, sink):
    blk = ATTN_BLOCK
    nb = seq // blk
    qw = N_Q_HEADS * HEAD_DIM
    kw = N_KV_HEADS * HEAD_DIM
    assert qw % kw == 0

    def kv_spec(col_blk, shift):
        return pl.BlockSpec((blk, kw), lambda b, n, s: (b * nb + shift(n), col_blk))

    prev = lambda n: jnp.maximum(n - 1, 0)
    ident = lambda n: n
    nxt = lambda n: jnp.minimum(n + 1, nb - 1)
    k_blk, v_blk = qw // kw, qw // kw + 1
    grid_spec = pltpu.PrefetchScalarGridSpec(
        num_scalar_prefetch=1,
        grid=(batch, nb),
        in_specs=[
            pl.BlockSpec((blk, qw), lambda b, n, s: (b * nb + n, 0)),
            kv_spec(k_blk, prev), kv_spec(k_blk, ident), kv_spec(k_blk, nxt),
            kv_spec(v_blk, prev), kv_spec(v_blk, ident), kv_spec(v_blk, nxt),
        ],
        out_specs=pl.BlockSpec((blk, qw), lambda b, n, s: (b * nb + n, 0)),
    )
    return pl.pallas_call(
        _attn_kernel,
        out_shape=jax.ShapeDtypeStruct((batch * seq, qw), BF16),
        grid_spec=grid_spec,
        compiler_params=_cparams(("parallel", "parallel"), 32 << 20),
        name="attention",
    )(sink, qkv, qkv, qkv, qkv, qkv, qkv, qkv)


def _cmul(ar, ai, br, bi):
    return ar * br - ai * bi, ar * bi + ai * br


def _cpow_table(br, bi, expo, nbits):
    pr = jnp.where((expo & 1) == 1, br, 1.0)
    pi = jnp.where((expo & 1) == 1, bi, 0.0)
    for k in range(1, nbits):
        br, bi = _cmul(br, bi, br, bi)
        bit = ((expo >> k) & 1) == 1
        nr, ni = _cmul(pr, pi, br, bi)
        pr = jnp.where(bit, nr, pr)
        pi = jnp.where(bit, ni, pi)
    return pr, pi


def _lambda_bar(lre, lim, log_dt):
    dt = jnp.exp(log_dt)
    mag = jnp.exp(lre * dt)
    return mag * jnp.cos(lim * dt), mag * jnp.sin(lim * dt)


def _ssm_weights(rowp_ref, colp_ref, btr_ref, bti_ref, crr_ref, cri_ref, ctr_ref, cti_ref, dcol_ref,
                 t_s, wz_s, wy_s, g_s):
    L, P, C = SSM_CHUNK, SSM_STATE, SSM_GROUP
    nbits = L.bit_length()

    lre, lim, ldt = rowp_ref[0, 0:1, :], rowp_ref[0, 1:2, :], rowp_ref[0, 2:3, :]
    lbr, lbi = _lambda_bar(lre, lim, ldt)
    den = lre * lre + lim * lim
    cfr = ((lbr - 1.0) * lre + lbi * lim) / den
    cfi = (lbi * lre - (lbr - 1.0) * lim) / den
    bbr, bbi = _cmul(cfr, cfi, btr_ref[0], bti_ref[0])

    dr, di = lbr, lbi
    for _ in range(L.bit_length() - 1):
        dr, di = _cmul(dr, di, dr, di)

    n_sub = lax.broadcasted_iota(jnp.int32, (L, 2 * P), 0)
    lane = lax.broadcasted_iota(jnp.int32, (L, 2 * P), 1)
    pzr, pzi = _cpow_table(lbr, lbi, jnp.where(lane < P, L - 1 - n_sub, n_sub), nbits)
    for c in range(C):
        zr, zi = _cmul(pzr, pzi, bbr[c:c + 1], bbi[c:c + 1])
        wz_s[c * L:(c + 1) * L, 0:2 * P] = zr.astype(BF16)
        wz_s[c * L:(c + 1) * L, 2 * P:4 * P] = zi.astype(BF16)

    b_rep_r = jnp.concatenate([jnp.broadcast_to(bbr[c:c + 1], (C, 2 * P)) for c in range(C)], axis=0)
    b_rep_i = jnp.concatenate([jnp.broadcast_to(bbi[c:c + 1], (C, 2 * P)) for c in range(C)], axis=0)
    c_tile_r = jnp.concatenate([crr_ref[0]] * C, axis=0)
    c_tile_i = jnp.concatenate([cri_ref[0]] * C, axis=0)
    bcr, bci = _cmul(b_rep_r, b_rep_i, c_tile_r, c_tile_i)

    clre, clim, cldt = colp_ref[0, :, 0:1], colp_ref[0, :, 1:2], colp_ref[0, :, 2:3]
    cbr, cbi = _lambda_bar(clre, clim, cldt)
    sub = lax.broadcasted_iota(jnp.int32, (2 * P, LANES), 0)
    m_lane = lax.broadcasted_iota(jnp.int32, (2 * P, LANES), 1)

    pgr, pgi = _cpow_table(cbr, cbi, jnp.abs(m_lane - (L - 1)), nbits)
    live = ((sub < P) & (m_lane >= L - 1)) | ((sub >= P) & (m_lane <= L - 1))
    pgr = jnp.where(live, pgr, 0.0)
    pgi = jnp.where(live, pgi, 0.0)
    g = _dot_exact(bcr, pgr) - _dot_exact(bci, pgi)
    g_row = lax.broadcasted_iota(jnp.int32, (C * C, LANES), 0)
    g_lane = lax.broadcasted_iota(jnp.int32, (C * C, LANES), 1)
    log2c = C.bit_length() - 1
    skip = ((g_row >> log2c) == (g_row & (C - 1))) & (g_lane == L - 1)
    g_s[...] = g + jnp.where(skip, dcol_ref[0], 0.0)

    j_lane = m_lane & (L - 1)
    pyr, pyi = _cpow_table(cbr, cbi, jnp.where(sub < P, j_lane + 1, L - j_lane), nbits)
    ctr, cti = ctr_ref[0], cti_ref[0]
    low = m_lane < L
    for a in range(C // 2):
        cr = jnp.where(low, ctr[:, 2 * a:2 * a + 1], ctr[:, 2 * a + 1:2 * a + 2])
        ci = jnp.where(low, cti[:, 2 * a:2 * a + 1], cti[:, 2 * a + 1:2 * a + 2])
        yr, yi = _cmul(cr, ci, pyr, pyi)
        wy_s[0:2 * P, a * LANES:(a + 1) * LANES] = yr.astype(BF16)
        wy_s[2 * P:4 * P, a * LANES:(a + 1) * LANES] = (-yi).astype(BF16)

    t_low = lax.broadcasted_iota(jnp.int32, (L, LANES), 1) < L

    for c_in in range(C):
        for a in range(C // 2):
            row = c_in * C + 2 * a
            ge = jnp.broadcast_to(g_s[row:row + 1, :], (L, LANES))
            go = jnp.broadcast_to(g_s[row + 1:row + 2, :], (L, LANES))
            te = pltpu.roll(ge, L + 1, 1, stride=1, stride_axis=0)
            to = pltpu.roll(go, 1, 1, stride=1, stride_axis=0)
            t_s[c_in * L:(c_in + 1) * L, a * LANES:(a + 1) * LANES] = jnp.where(t_low, te, to).astype(BF16)
    return dr, di


def _ssm_kernel(*refs, seqs):
    P, L, C = SSM_STATE, SSM_CHUNK, SSM_GROUP
    ns = len(seqs)
    u_refs, refs = refs[:ns], refs[ns:]
    param_refs, refs = refs[:9], refs[9:]
    y_refs, refs = refs[:ns], refs[ns:]
    t_s, wz_s, wy_s, g_s, a_s, yt_s, zr_s, zi_s, fr_s, fi_s, rr_s, ri_s = refs
    blocks = [nseq * (slen // TOK_BLOCK) for nseq, slen in seqs]
    nblk = sum(blocks)
    ar, ai = _ssm_weights(*param_refs, t_s, wz_s, wy_s, g_s)

    blk0 = 0
    for u_ref, nb in zip(u_refs, blocks):
        a_s[blk0 * C:(blk0 + nb) * C, :] = u_ref[...].astype(F32).reshape(nb * C, TOK_BLOCK)
        blk0 += nb

    low = lax.broadcasted_iota(jnp.int32, (nblk, LANES), 1) < L
    first, second = [], []
    for a in range(C // 2):
        x0 = a_s[pl.ds(2 * a, nblk, stride=C), :]
        x1 = a_s[pl.ds(2 * a + 1, nblk, stride=C), :]
        first.append(jnp.where(low, x0, pltpu.roll(x1, L, 1)))
        second.append(jnp.where(low, pltpu.roll(x0, L, 1), x1))
    u = jnp.concatenate([jnp.concatenate(first, axis=1), jnp.concatenate(second, axis=1)], axis=0).astype(BF16)

    z = _dot(u, wz_s[...])
    zr_s[...] = z[:, 0:2 * P]
    zi_s[...] = z[:, 2 * P:4 * P]

    def scan(blk0, nseq, n):
        is_fwd = lax.broadcasted_iota(jnp.int32, (nseq, 2 * P), 1) < P

        def rows(block, second_half):
            return pl.ds(blk0 + block + (nblk if second_half else 0), nseq, stride=n)

        def half(carry, f_rows, b_rows):
            sr, si = carry
            fr_s[f_rows, :] = sr
            fi_s[f_rows, :] = si
            rr_s[b_rows, :] = sr
            ri_s[b_rows, :] = si
            zr = jnp.where(is_fwd, zr_s[f_rows, :], zr_s[b_rows, :])
            zi = jnp.where(is_fwd, zi_s[f_rows, :], zi_s[b_rows, :])
            return ar * sr - ai * si + zr, ar * si + ai * sr + zi

        def step(t, carry):
            carry = half(carry, rows(t, False), rows(n - 1 - t, True))
            return half(carry, rows(t, True), rows(n - 1 - t, False))

        zero = jnp.zeros((nseq, 2 * P), F32)
        lax.fori_loop(0, n, step, (zero, zero))

    blk0 = 0
    for (nseq, slen), nb in zip(seqs, blocks):
        scan(blk0, nseq, slen // TOK_BLOCK)
        blk0 += nb

    sel = lax.broadcasted_iota(jnp.int32, fr_s.shape, 1) < P
    s_in = jnp.concatenate([jnp.where(sel, fr_s[...], rr_s[...]),
                            jnp.where(sel, fi_s[...], ri_s[...])], axis=1).astype(BF16)
    y = _dot(u, t_s[...]) + _dot(s_in, wy_s[...])

    for a in range(C // 2):
        y0 = y[0:nblk, a * LANES:(a + 1) * LANES]
        y1 = y[nblk:2 * nblk, a * LANES:(a + 1) * LANES]
        yt_s[pl.ds(2 * a, nblk, stride=C), :] = jnp.where(low, y0, pltpu.roll(y1, L, 1))
        yt_s[pl.ds(2 * a + 1, nblk, stride=C), :] = jnp.where(low, pltpu.roll(y0, L, 1), y1)
    blk0 = 0
    for y_ref, nb in zip(y_refs, blocks):
        y_ref[...] = yt_s[blk0 * C:(blk0 + nb) * C, :].reshape(nb, C, TOK_BLOCK).astype(y_ref.dtype)
        blk0 += nb


def _ssm(uts, seqs, lam_re, lam_im, log_dt, b_re, b_im, c_re, c_im, d_skip):
    width = uts[0].shape[1]
    _, groups, P = lam_re.shape
    C = SSM_GROUP
    assert P == SSM_STATE and b_re.shape[-1] == C and groups * C == width
    assert all(s % TOK_BLOCK == 0 for _, s in seqs)
    assert all(ut.shape == (b * s // TOK_BLOCK, width, TOK_BLOCK) for ut, (b, s) in zip(uts, seqs))
    nblk = sum(ut.shape[0] for ut in uts)

    def lanes_fb(a):
        return jnp.transpose(a, (1, 0, 2)).reshape(groups, 2 * P)

    ldt = jnp.broadcast_to(log_dt[:, :, None], (2, groups, P))
    rowp = jnp.stack([lanes_fb(lam_re), lanes_fb(lam_im), lanes_fb(ldt)], axis=1)
    colp = jnp.transpose(rowp, (0, 2, 1))
    bt = lambda b: jnp.transpose(b, (1, 3, 0, 2)).reshape(groups, C, 2 * P)
    cr = lambda c: jnp.transpose(c, (1, 2, 0, 3)).reshape(groups, C, 2 * P)
    ct = lambda c: jnp.transpose(c, (1, 0, 3, 2)).reshape(groups, 2 * P, C)
    dcol = jnp.tile(d_skip.reshape(groups, 1, C), (1, C, 1)).reshape(groups, C * C, 1)

    g3 = lambda s1, s2: pl.BlockSpec((1, s1, s2), lambda g: (g, 0, 0))
    acts = [pl.BlockSpec((ut.shape[0], C, TOK_BLOCK), lambda g: (0, g, 0)) for ut in uts]
    rows2 = 2 * nblk
    return pl.pallas_call(
        functools.partial(_ssm_kernel, seqs=tuple(seqs)),
        out_shape=[jax.ShapeDtypeStruct(ut.shape, BF16) for ut in uts],
        grid=(groups,),
        in_specs=acts + [g3(3, 2 * P), g3(2 * P, 3), g3(C, 2 * P), g3(C, 2 * P), g3(C, 2 * P), g3(C, 2 * P),
                         g3(2 * P, C), g3(2 * P, C), g3(C * C, 1)],
        out_specs=acts,
        scratch_shapes=[
            pltpu.VMEM((CHUNK_W, CHUNK_W), BF16),
            pltpu.VMEM((CHUNK_W, 4 * P), BF16),
            pltpu.VMEM((4 * P, CHUNK_W), BF16),
            pltpu.VMEM((C * C, LANES), F32),
            pltpu.VMEM((nblk * C, TOK_BLOCK), F32),
            pltpu.VMEM((nblk * C, TOK_BLOCK), F32),
        ] + [pltpu.VMEM((rows2, 2 * P), F32)] * 6,
        compiler_params=_cparams(("parallel",), 40 << 20),
        name="ssm",
    )(*uts, rowp, colp, bt(b_re), bt(b_im), cr(c_re), cr(c_im), ct(c_re), ct(c_im), dcol)


def _merge_kernel(x_ref, a_ref, yt_ref, ga_ref, gs_ref, wglu_ref, bglu_ref, wao_ref, wso_ref, wout_ref, o_ref):
    y = jnp.concatenate([yt_ref[j].astype(F32).T for j in range(yt_ref.shape[0])], axis=0)
    z = jax.nn.gelu(y)
    z = z * jax.nn.sigmoid(_dot(z.astype(BF16), wglu_ref[...]) + bglu_ref[...])
    m = _dot(z.astype(BF16), wso_ref[...])
    a = _dot(a_ref[...], wao_ref[...])
    merged = jax.nn.sigmoid(ga_ref[...].astype(F32)) * a + jax.nn.sigmoid(gs_ref[...].astype(F32)) * m
    o_ref[...] = x_ref[...] + _dot(merged.astype(BF16), wout_ref[...])


def _merge(x, attn, yt, gates, w_glu, b_glu, w_ao, w_so, w_out):
    t, d = x.shape
    aw, sw = attn.shape[1], yt.shape[1]
    tm = MERGE_ROWS
    weights = (sw * sw + aw * d + sw * d + d * d) * 2
    vmem = weights + 2 * tm * (2 * d * 4 + (aw + sw + 2 * d) * 2) + 6 * tm * d * 4 + (4 << 20)
    return pl.pallas_call(
        _merge_kernel,
        out_shape=jax.ShapeDtypeStruct((t, d), F32),
        grid=(t // tm,),
        in_specs=[
            pl.BlockSpec((tm, d), lambda i: (i, 0)),
            pl.BlockSpec((tm, aw), lambda i: (i, 0)),
            pl.BlockSpec((tm // TOK_BLOCK, sw, TOK_BLOCK), lambda i: (i, 0, 0)),
            pl.BlockSpec((tm, d), lambda i: (i, 0)),
            pl.BlockSpec((tm, d), lambda i: (i, 1)),
            _const_spec((sw, sw)), _const_spec((1, sw)), _const_spec((aw, d)), _const_spec((sw, d)),
            _const_spec((d, d)),
        ],
        out_specs=pl.BlockSpec((tm, d), lambda i: (i, 0)),
        compiler_params=_cparams(("parallel",), vmem),
        name="mixer_out",
    )(x, attn, yt, gates, gates, w_glu, b_glu.reshape(1, sw), w_ao, w_so, w_out)


def _layer(xs, ffn1_norm, ffn1_w_gate_up, ffn1_w_down, mix_norm, w_in, q_norm, k_norm, attn_sink,
           ssm_lambda_re, ssm_lambda_im, ssm_log_dt, ssm_b_re, ssm_b_im, ssm_c_re, ssm_c_im, ssm_d,
           w_glu, b_glu, w_attn_out, w_ssm_out, w_out, ffn2_norm, ffn2_w_gate_up, ffn2_w_down):
    d = xs[0].shape[-1]
    seqs = [(x.shape[0], x.shape[1]) for x in xs]
    aw = N_Q_HEADS * HEAD_DIM
    kw = N_KV_HEADS * HEAD_DIM
    sw = ssm_d.shape[0]

    o_u, o_g = aw + 2 * kw, aw + 2 * kw + sw
    w_qkv = w_in[:, :o_u].astype(BF16)
    w_u_t = w_in[:, o_u:o_g].T.astype(BF16)
    w_gates = w_in[:, o_g:].astype(BF16)
    w1_gu, w1_d = ffn1_w_gate_up.astype(BF16), ffn1_w_down.astype(BF16)
    w2_gu, w2_d = ffn2_w_gate_up.astype(BF16), ffn2_w_down.astype(BF16)
    w_glu_b, w_ao, w_so, w_o = (w.astype(BF16) for w in (w_glu, w_attn_out, w_ssm_out, w_out))
    cos2, sin2 = _rope_tables(max(s for _, s in seqs))

    x1s, attns, gates, uts = [], [], [], []
    for x, (b, s) in zip(xs, seqs):
        x1 = _ffn(x.reshape(b * s, d), ffn1_norm, w1_gu, w1_d)
        tm = math.gcd(INPROJ_ROWS, s)
        qkv = _inproj_qkv(x1, mix_norm, w_qkv, tm, s, cos2, sin2, q_norm, k_norm)
        uts.append(_inproj_u(x1, mix_norm, w_u_t, tm))
        gates.append(_inproj_gates(x1, mix_norm, w_gates, tm))
        attns.append(_attention(qkv, b, s, attn_sink))
        x1s.append(x1)

    yts = _ssm(uts, seqs, ssm_lambda_re, ssm_lambda_im, ssm_log_dt, ssm_b_re, ssm_b_im, ssm_c_re, ssm_c_im, ssm_d)

    outs = []
    for x_in, x1, attn, yt, gate in zip(xs, x1s, attns, yts, gates):
        x2 = _merge(x1, attn, yt, gate, w_glu_b, b_glu, w_ao, w_so, w_o)
        outs.append(_ffn(x2, ffn2_norm, w2_gu, w2_d).reshape(x_in.shape))
    return tuple(outs)


def kernel(x_prompt, x_sample, ffn1_norm, ffn1_w_gate_up, ffn1_w_down, mix_norm, w_in, q_norm, k_norm, attn_sink, ssm_lambda_re, ssm_lambda_im, ssm_log_dt, ssm_b_re, ssm_b_im, ssm_c_re, ssm_c_im, ssm_d, w_glu, b_glu, w_attn_out, w_ssm_out, w_out, ffn2_norm, ffn2_w_gate_up, ffn2_w_down):
    xs = (x_prompt, x_sample)
    depth = ffn1_norm.shape[0]
    params = (ffn1_norm, ffn1_w_gate_up, ffn1_w_down, mix_norm, w_in, q_norm, k_norm, attn_sink,
              ssm_lambda_re, ssm_lambda_im, ssm_log_dt, ssm_b_re, ssm_b_im, ssm_c_re, ssm_c_im, ssm_d,
              w_glu, b_glu, w_attn_out, w_ssm_out, w_out, ffn2_norm, ffn2_w_gate_up, ffn2_w_down)
    for l in range(depth):
        xs = _layer(xs, *(p[l] for p in params))
    return xs
```

```python
import functools
import math

import jax
import jax.numpy as jnp
import numpy as np
from jax import lax
from jax.experimental import pallas as pl
from jax.experimental.pallas import tpu as pltpu

F32 = jnp.float32
BF16 = jnp.bfloat16

HEAD_DIM = 128
N_Q_HEADS = 8
N_KV_HEADS = 2
Q_PER_KV = N_Q_HEADS // N_KV_HEADS
WINDOW = 128
ATTN_BLOCK = 128
ROPE_THETA = 10000.0
SSM_GROUP = 16
SSM_STATE = 64
EPS = 1e-6
LOG2E = math.log2(math.e)

LANES = 128
VMEM_LIMIT_CAP = 56 * 1024 * 1024

SSM_CHUNK = 64
CHUNK_W = SSM_CHUNK * SSM_GROUP
TOK_BLOCK = 128
FFN_ROWS = 512
FFN_COLS = 512
INPROJ_ROWS = 512
MERGE_ROWS = 256

assert 2 * SSM_CHUNK == TOK_BLOCK == LANES and 2 * SSM_STATE == LANES


def _cparams(semantics, vmem_bytes):
    return pltpu.CompilerParams(dimension_semantics=semantics,
                                vmem_limit_bytes=min(int(vmem_bytes), VMEM_LIMIT_CAP))


def _rms(x, gain):
    return x * lax.rsqrt(jnp.mean(x * x, axis=-1, keepdims=True) + EPS) * gain


def _dot(a, b):
    return jnp.dot(a, b, preferred_element_type=F32)


def _dot_exact(a, b):
    return jnp.dot(a, b, preferred_element_type=F32, precision=lax.Precision.HIGHEST)


def _ffn_kernel(x_ref, gain_ref, wgu_ref, wd_ref, o_ref, h_ref):
    @pl.when(pl.program_id(1) == 0)
    def _():
        x = x_ref[...]
        h_ref[...] = _rms(x, gain_ref[...]).astype(BF16)
        o_ref[...] = x

    r = _dot(h_ref[...], wgu_ref[0])
    tf = r.shape[1] // 2
    g, u = r[:, :tf], r[:, tf:]
    a = (g * jax.nn.sigmoid(g)) * u
    o_ref[...] += _dot(a.astype(BF16), wd_ref[...])


def _ffn_weights(w_gate_up, w_down):
    d, d_ff = w_gate_up.shape[0], w_down.shape[0]
    tf = FFN_COLS
    nf = d_ff // tf
    gate = w_gate_up[:, :d_ff].reshape(d, nf, tf)
    up = w_gate_up[:, d_ff:].reshape(d, nf, tf)
    w_gu = jnp.transpose(jnp.concatenate([gate, up], axis=2), (1, 0, 2)).astype(BF16)
    return w_gu, (0.5 * w_down).astype(BF16)


def _ffn(x, gain, w_gu, w_down_half):
    t, d = x.shape
    nf, _, tf2 = w_gu.shape
    tf = tf2 // 2
    tm = FFN_ROWS
    vmem = (4 * tm * d * 4) + (tm * d * 2) + 2 * 3 * (d * tf * 2) + 4 * (tm * tf * 4) + (4 << 20)
    return pl.pallas_call(
        _ffn_kernel,
        out_shape=jax.ShapeDtypeStruct((t, d), F32),
        grid=(t // tm, nf),
        in_specs=[
            pl.BlockSpec((tm, d), lambda i, f: (i, 0)),
            pl.BlockSpec((1, d), lambda i, f: (0, 0)),
            pl.BlockSpec((1, d, tf2), lambda i, f: (f, 0, 0)),
            pl.BlockSpec((tf, d), lambda i, f: (f, 0)),
        ],
        out_specs=pl.BlockSpec((tm, d), lambda i, f: (i, 0)),
        scratch_shapes=[pltpu.VMEM((tm, d), BF16)],
        compiler_params=_cparams(("parallel", "arbitrary"), vmem),
        name="ffn",
    )(x, gain.reshape(1, d), w_gu, w_down_half)


def _inproj_gates_kernel(x_ref, gain_ref, w_ref, o_ref):
    h = _rms(x_ref[...], gain_ref[...]).astype(BF16)
    o_ref[...] = _dot(h, w_ref[...]).astype(BF16)


def _rope(x, cos2, sin2):
    return x * cos2 + pltpu.roll(x, HEAD_DIM // 2, 1) * sin2


def _inproj_qkv_kernel(x_ref, gain_ref, w_ref, cos_ref, sin_ref, qg_ref, kg_ref, o_ref):
    qg = qg_ref[...] * (LOG2E * HEAD_DIM ** -0.5)
    kg = kg_ref[...]
    n_rot = N_Q_HEADS + N_KV_HEADS
    half_rows = x_ref.shape[0] // 2
    for half in range(2):
        rows = slice(half * half_rows, (half + 1) * half_rows)
        h = _rms(x_ref[rows, :], gain_ref[...]).astype(BF16)
        r = _dot(h, w_ref[...])
        cos2, sin2 = cos_ref[rows, :], sin_ref[rows, :]
        for head in range(n_rot):
            cols = slice(head * HEAD_DIM, (head + 1) * HEAD_DIM)
            gain = qg if head < N_Q_HEADS else kg
            o_ref[rows, cols] = _rope(_rms(r[:, cols], gain), cos2, sin2).astype(BF16)
        vcols = slice(n_rot * HEAD_DIM, (n_rot + N_KV_HEADS) * HEAD_DIM)
        o_ref[rows, vcols] = r[:, vcols].astype(BF16)


def _inproj_u_kernel(x_ref, gain_ref, wt_ref, o_ref):
    h = _rms(x_ref[...], gain_ref[...]).astype(BF16)
    rt = lax.dot_general(wt_ref[...], h, (((1,), (1,)), ((), ())), preferred_element_type=F32)
    for j in range(o_ref.shape[0]):
        o_ref[j] = rt[:, j * TOK_BLOCK:(j + 1) * TOK_BLOCK].astype(BF16)


def _const_spec(shape):
    return pl.BlockSpec(shape, lambda *_: (0,) * len(shape), pipeline_mode=pl.Buffered(1))


def _inproj_gates(x, gain, w, tm):
    t, d = x.shape
    n = w.shape[1]
    tn = n // 2
    vmem = 2 * (tm * d * 4) + 2 * (d * tn * 2) + 2 * (tm * tn * 2) + (tm * tn * 4) + (tm * d * 6) + (4 << 20)
    return pl.pallas_call(
        _inproj_gates_kernel,
        out_shape=jax.ShapeDtypeStruct((t, n), BF16),
        grid=(2, t // tm),
        in_specs=[
            pl.BlockSpec((tm, d), lambda j, i: (i, 0)),
            pl.BlockSpec((1, d), lambda j, i: (0, 0)),
            pl.BlockSpec((d, tn), lambda j, i: (0, j)),
        ],
        out_specs=pl.BlockSpec((tm, tn), lambda j, i: (i, j)),
        compiler_params=_cparams(("parallel", "parallel"), vmem),
        name="inproj_gates",
    )(x, gain.reshape(1, d), w)


def _inproj_qkv(x, gain, w, tm, seq, cos2, sin2, q_gain, k_gain):
    t, d = x.shape
    n = w.shape[1]
    tiles_per_seq = seq // tm
    vmem = 2 * (tm * d * 4) + (d * n * 2) + 2 * (tm * n * 2) + 2 * (tm * n * 4) + (tm * d * 6) + (6 << 20)
    return pl.pallas_call(
        _inproj_qkv_kernel,
        out_shape=jax.ShapeDtypeStruct((t, n), BF16),
        grid=(t // tm,),
        in_specs=[
            pl.BlockSpec((tm, d), lambda i: (i, 0)),
            _const_spec((1, d)),
            _const_spec((d, n)),
            pl.BlockSpec((tm, HEAD_DIM), lambda i: (i % tiles_per_seq, 0)),
            pl.BlockSpec((tm, HEAD_DIM), lambda i: (i % tiles_per_seq, 0)),
            _const_spec((1, HEAD_DIM)),
            _const_spec((1, HEAD_DIM)),
        ],
        out_specs=pl.BlockSpec((tm, n), lambda i: (i, 0)),
        compiler_params=_cparams(("parallel",), vmem),
        name="inproj_qkv",
    )(x, gain.reshape(1, d), w, cos2, sin2, q_gain.reshape(1, HEAD_DIM), k_gain.reshape(1, HEAD_DIM))


def _inproj_u(x, gain, wt, tm):
    t, d = x.shape
    n = wt.shape[0]
    vmem = 2 * (tm * d * 4) + (d * n * 2) + 2 * (tm * n * 2) + 2 * (tm * n * 4) + (tm * d * 6) + (6 << 20)
    return pl.pallas_call(
        _inproj_u_kernel,
        out_shape=jax.ShapeDtypeStruct((t // TOK_BLOCK, n, TOK_BLOCK), BF16),
        grid=(t // tm,),
        in_specs=[
            pl.BlockSpec((tm, d), lambda i: (i, 0)),
            _const_spec((1, d)),
            _const_spec((n, d)),
        ],
        out_specs=pl.BlockSpec((tm // TOK_BLOCK, n, TOK_BLOCK), lambda i: (i, 0, 0)),
        compiler_params=_cparams(("parallel",), vmem),
        name="inproj_u",
    )(x, gain.reshape(1, d), wt)


def _rope_table_kernel(freq_ref, cos_ref, sin_ref):
    rows = cos_ref.shape[0]
    pos = (lax.broadcasted_iota(jnp.int32, (rows, HEAD_DIM), 0) + pl.program_id(0) * rows).astype(F32)
    ang = pos * freq_ref[...]
    lane = lax.broadcasted_iota(jnp.int32, (rows, HEAD_DIM), 1)
    cos_ref[...] = jnp.cos(ang)
    sin_ref[...] = jnp.where(lane < HEAD_DIM // 2, -1.0, 1.0) * jnp.sin(ang)


def _rope_tables(seq):
    inv_freq = ROPE_THETA ** (-jnp.arange(0, HEAD_DIM, 2, dtype=F32) / HEAD_DIM)
    freq2 = jnp.concatenate([inv_freq, inv_freq]).reshape(1, HEAD_DIM)
    rows = math.gcd(seq, 512)
    return pl.pallas_call(
        _rope_table_kernel,
        out_shape=(jax.ShapeDtypeStruct((seq, HEAD_DIM), F32),) * 2,
        grid=(seq // rows,),
        in_specs=[pl.BlockSpec((1, HEAD_DIM), lambda i: (0, 0))],
        out_specs=(pl.BlockSpec((rows, HEAD_DIM), lambda i: (i, 0)),) * 2,
        compiler_params=_cparams(("parallel",), 16 << 20),
        name="rope_tables",
    )(freq2)


def _attn_bias(nq):
    blk = ATTN_BLOCK
    kj = np.arange(3 * blk)[:, None]
    qi = np.arange(nq)[None, :] % blk
    ok = np.abs(kj - blk - qi) <= WINDOW
    variants = []
    for v in range(4):
        valid = ok & ((kj >= blk) | (v & 1 == 0)) & ((kj < 2 * blk) | (v & 2 == 0))
        variants.append(np.where(valid, 0.0, -np.inf))
    return jnp.asarray(np.stack(variants), F32)


def _attn_kernel(sink_ref, bias_ref, q_ref, kvp_ref, kvc_ref, kvn_ref, o_ref):
    n = pl.program_id(1)
    blk = ATTN_BLOCK
    kw = N_KV_HEADS * HEAD_DIM
    biases = (bias_ref[jnp.where(n == 0, 1, 0)], bias_ref[jnp.where(n == pl.num_programs(1) - 1, 2, 0)])

    for h in range(N_KV_HEADS):
        kc = slice(h * HEAD_DIM, (h + 1) * HEAD_DIM)
        vc = slice(kw + h * HEAD_DIM, kw + (h + 1) * HEAD_DIM)
        k4 = (kvp_ref[:, kc], kvc_ref[0:blk, kc], kvc_ref[blk:2 * blk, kc], kvn_ref[:, kc])
        v4 = (kvp_ref[:, vc], kvc_ref[0:blk, vc], kvc_ref[blk:2 * blk, vc], kvn_ref[:, vc])
        heads = range(h * Q_PER_KV, (h + 1) * Q_PER_KV)
        sink = jnp.concatenate([jnp.full((1, blk), sink_ref[hd] * LOG2E, F32) for hd in heads], axis=1)
        for qb in range(2):
            rows = slice(qb * blk, (qb + 1) * blk)
            k_band = jnp.concatenate(k4[qb:qb + 3], axis=0)
            v_band = jnp.concatenate(v4[qb:qb + 3], axis=0)
            q_all = jnp.concatenate([q_ref[rows, hd * HEAD_DIM:(hd + 1) * HEAD_DIM] for hd in heads], axis=0)
            s = lax.dot_general(k_band, q_all, (((1,), (1,)), ((), ())), preferred_element_type=F32) + biases[qb]
            m = jnp.maximum(jnp.max(s, axis=0, keepdims=True), sink)
            p = jnp.exp2(s - m)
            denom = jnp.sum(p, axis=0, keepdims=True) + jnp.exp2(sink - m)
            ot = lax.dot_general(v_band, p.astype(BF16), (((0,), (0,)), ((), ())), preferred_element_type=F32)
            ot = ot / denom
            for g, hd in enumerate(heads):
                o_ref[rows, hd * HEAD_DIM:(hd + 1) * HEAD_DIM] = ot[:, g * blk:(g + 1) * blk].T.astype(BF16)


def _attention(qkv, batch, seq, sink):
    blk = ATTN_BLOCK
    assert seq % (2 * blk) == 0
    nb = seq // (2 * blk)
    qw = N_Q_HEADS * HEAD_DIM
    kvw = 2 * N_KV_HEADS * HEAD_DIM
    assert qw % kvw == 0
    kv_col = qw // kvw
    nq = Q_PER_KV * blk
    grid_spec = pltpu.PrefetchScalarGridSpec(
        num_scalar_prefetch=1,
        grid=(batch, nb),
        in_specs=[
            _const_spec((4, 3 * blk, nq)),
            pl.BlockSpec((2 * blk, qw), lambda b, n, s: (b * nb + n, 0)),
            pl.BlockSpec((blk, kvw), lambda b, n, s: (2 * (b * nb) + jnp.maximum(2 * n - 1, 0), kv_col)),
            pl.BlockSpec((2 * blk, kvw), lambda b, n, s: (b * nb + n, kv_col)),
            pl.BlockSpec((blk, kvw), lambda b, n, s: (2 * (b * nb) + jnp.minimum(2 * n + 2, 2 * nb - 1), kv_col)),
        ],
        out_specs=pl.BlockSpec((2 * blk, qw), lambda b, n, s: (b * nb + n, 0)),
    )
    return pl.pallas_call(
        _attn_kernel,
        out_shape=jax.ShapeDtypeStruct((batch * seq, qw), BF16),
        grid_spec=grid_spec,
        compiler_params=_cparams(("parallel", "parallel"), 32 << 20),
        name="attention",
    )(sink, _attn_bias(nq), qkv, qkv, qkv, qkv)


def _cmul(ar, ai, br, bi):
    return ar * br - ai * bi, ar * bi + ai * br


def _cpow_table(br, bi, expo, nbits):
    pr = jnp.where((expo & 1) == 1, br, 1.0)
    pi = jnp.where((expo & 1) == 1, bi, 0.0)
    for k in range(1, nbits):
        br, bi = _cmul(br, bi, br, bi)
        bit = ((expo >> k) & 1) == 1
        nr, ni = _cmul(pr, pi, br, bi)
        pr = jnp.where(bit, nr, pr)
        pi = jnp.where(bit, ni, pi)
    return pr, pi


def _lambda_bar(lre, lim, log_dt):
    dt = jnp.exp(log_dt)
    mag = jnp.exp(lre * dt)
    return mag * jnp.cos(lim * dt), mag * jnp.sin(lim * dt)


def _ssm_weights(rowp_ref, colp_ref, btr_ref, bti_ref, crr_ref, cri_ref, ctr_ref, cti_ref, dcol_ref,
                 t_s, wz_s, wy_s, g_s):
    L, P, C = SSM_CHUNK, SSM_STATE, SSM_GROUP
    nbits = L.bit_length()

    lre, lim, ldt = rowp_ref[0, 0:1, :], rowp_ref[0, 1:2, :], rowp_ref[0, 2:3, :]
    lbr, lbi = _lambda_bar(lre, lim, ldt)
    den = lre * lre + lim * lim
    cfr = ((lbr - 1.0) * lre + lbi * lim) / den
    cfi = (lbi * lre - (lbr - 1.0) * lim) / den
    bbr, bbi = _cmul(cfr, cfi, btr_ref[0], bti_ref[0])

    dr, di = lbr, lbi
    for _ in range(L.bit_length() - 1):
        dr, di = _cmul(dr, di, dr, di)

    n_sub = lax.broadcasted_iota(jnp.int32, (L, 2 * P), 0)
    lane = lax.broadcasted_iota(jnp.int32, (L, 2 * P), 1)
    pzr, pzi = _cpow_table(lbr, lbi, jnp.where(lane < P, L - 1 - n_sub, n_sub), nbits)
    for c in range(C):
        zr, zi = _cmul(pzr, pzi, bbr[c:c + 1], bbi[c:c + 1])
        wz_s[c * L:(c + 1) * L, 0:2 * P] = zr.astype(BF16)
        wz_s[c * L:(c + 1) * L, 2 * P:4 * P] = zi.astype(BF16)

    b_rep_r = jnp.concatenate([jnp.broadcast_to(bbr[c:c + 1], (C, 2 * P)) for c in range(C)], axis=0)
    b_rep_i = jnp.concatenate([jnp.broadcast_to(bbi[c:c + 1], (C, 2 * P)) for c in range(C)], axis=0)
    c_tile_r = jnp.concatenate([crr_ref[0]] * C, axis=0)
    c_tile_i = jnp.concatenate([cri_ref[0]] * C, axis=0)
    bcr, bci = _cmul(b_rep_r, b_rep_i, c_tile_r, c_tile_i)

    clre, clim, cldt = colp_ref[0, :, 0:1], colp_ref[0, :, 1:2], colp_ref[0, :, 2:3]
    cbr, cbi = _lambda_bar(clre, clim, cldt)
    sub = lax.broadcasted_iota(jnp.int32, (2 * P, LANES), 0)
    m_lane = lax.broadcasted_iota(jnp.int32, (2 * P, LANES), 1)

    pgr, pgi = _cpow_table(cbr, cbi, jnp.abs(m_lane - (L - 1)), nbits)
    live = ((sub < P) & (m_lane >= L - 1)) | ((sub >= P) & (m_lane <= L - 1))
    pgr = jnp.where(live, pgr, 0.0)
    pgi = jnp.where(live, pgi, 0.0)
    g = _dot_exact(bcr, pgr) - _dot_exact(bci, pgi)
    g_row = lax.broadcasted_iota(jnp.int32, (C * C, LANES), 0)
    g_lane = lax.broadcasted_iota(jnp.int32, (C * C, LANES), 1)
    log2c = C.bit_length() - 1
    skip = ((g_row >> log2c) == (g_row & (C - 1))) & (g_lane == L - 1)
    g_s[...] = g + jnp.where(skip, dcol_ref[0], 0.0)

    j_lane = m_lane & (L - 1)
    pyr, pyi = _cpow_table(cbr, cbi, jnp.where(sub < P, j_lane + 1, L - j_lane), nbits)
    ctr, cti = ctr_ref[0], cti_ref[0]
    low = m_lane < L
    for a in range(C // 2):
        cr = jnp.where(low, ctr[:, 2 * a:2 * a + 1], ctr[:, 2 * a + 1:2 * a + 2])
        ci = jnp.where(low, cti[:, 2 * a:2 * a + 1], cti[:, 2 * a + 1:2 * a + 2])
        yr, yi = _cmul(cr, ci, pyr, pyi)
        wy_s[0:2 * P, a * LANES:(a + 1) * LANES] = yr.astype(BF16)
        wy_s[2 * P:4 * P, a * LANES:(a + 1) * LANES] = (-yi).astype(BF16)

    t_low = lax.broadcasted_iota(jnp.int32, (L, LANES), 1) < L

    for c_in in range(C):
        for a in range(C // 2):
            row = c_in * C + 2 * a
            ge = jnp.broadcast_to(g_s[row:row + 1, :], (L, LANES))
            go = jnp.broadcast_to(g_s[row + 1:row + 2, :], (L, LANES))
            te = pltpu.roll(ge, L + 1, 1, stride=1, stride_axis=0)
            to = pltpu.roll(go, 1, 1, stride=1, stride_axis=0)
            t_s[c_in * L:(c_in + 1) * L, a * LANES:(a + 1) * LANES] = jnp.where(t_low, te, to).astype(BF16)
    return dr, di


def _ssm_kernel(*refs, seqs):
    P, L, C = SSM_STATE, SSM_CHUNK, SSM_GROUP
    ns = len(seqs)
    u_refs, refs = refs[:ns], refs[ns:]
    param_refs, refs = refs[:9], refs[9:]
    y_refs, refs = refs[:ns], refs[ns:]
    t_s, wz_s, wy_s, g_s, a_s, yt_s, zr_s, zi_s, fr_s, fi_s, rr_s, ri_s = refs
    blocks = [nseq * (slen // TOK_BLOCK) for nseq, slen in seqs]
    nblk = sum(blocks)
    ar, ai = _ssm_weights(*param_refs, t_s, wz_s, wy_s, g_s)

    blk0 = 0
    for u_ref, nb in zip(u_refs, blocks):
        a_s[blk0 * C:(blk0 + nb) * C, :] = u_ref[...].astype(F32).reshape(nb * C, TOK_BLOCK)
        blk0 += nb

    low = lax.broadcasted_iota(jnp.int32, (nblk, LANES), 1) < L
    first, second = [], []
    for a in range(C // 2):
        x0 = a_s[pl.ds(2 * a, nblk, stride=C), :]
        x1 = a_s[pl.ds(2 * a + 1, nblk, stride=C), :]
        first.append(jnp.where(low, x0, pltpu.roll(x1, L, 1)))
        second.append(jnp.where(low, pltpu.roll(x0, L, 1), x1))
    u = jnp.concatenate([jnp.concatenate(first, axis=1), jnp.concatenate(second, axis=1)], axis=0).astype(BF16)

    z = _dot(u, wz_s[...])
    zr_s[...] = z[:, 0:2 * P]
    zi_s[...] = z[:, 2 * P:4 * P]

    def scan(blk0, nseq, n):
        is_fwd = lax.broadcasted_iota(jnp.int32, (nseq, 2 * P), 1) < P

        def rows(block, second_half):
            return pl.ds(blk0 + block + (nblk if second_half else 0), nseq, stride=n)

        def half(carry, f_rows, b_rows):
            sr, si = carry
            fr_s[f_rows, :] = sr
            fi_s[f_rows, :] = si
            rr_s[b_rows, :] = sr
            ri_s[b_rows, :] = si
            zr = jnp.where(is_fwd, zr_s[f_rows, :], zr_s[b_rows, :])
            zi = jnp.where(is_fwd, zi_s[f_rows, :], zi_s[b_rows, :])
            return ar * sr - ai * si + zr, ar * si + ai * sr + zi

        def step(t, carry):
            carry = half(carry, rows(t, False), rows(n - 1 - t, True))
            return half(carry, rows(t, True), rows(n - 1 - t, False))

        zero = jnp.zeros((nseq, 2 * P), F32)
        lax.fori_loop(0, n, step, (zero, zero))

    blk0 = 0
    for (nseq, slen), nb in zip(seqs, blocks):
        scan(blk0, nseq, slen // TOK_BLOCK)
        blk0 += nb

    sel = lax.broadcasted_iota(jnp.int32, fr_s.shape, 1) < P
    s_in = jnp.concatenate([jnp.where(sel, fr_s[...], rr_s[...]),
                            jnp.where(sel, fi_s[...], ri_s[...])], axis=1).astype(BF16)
    y = _dot(u, t_s[...]) + _dot(s_in, wy_s[...])

    for a in range(C // 2):
        y0 = y[0:nblk, a * LANES:(a + 1) * LANES]
        y1 = y[nblk:2 * nblk, a * LANES:(a + 1) * LANES]
        yt_s[pl.ds(2 * a, nblk, stride=C), :] = jnp.where(low, y0, pltpu.roll(y1, L, 1))
        yt_s[pl.ds(2 * a + 1, nblk, stride=C), :] = jnp.where(low, pltpu.roll(y0, L, 1), y1)
    blk0 = 0
    for y_ref, nb in zip(y_refs, blocks):
        y_ref[...] = yt_s[blk0 * C:(blk0 + nb) * C, :].reshape(nb, C, TOK_BLOCK).astype(y_ref.dtype)
        blk0 += nb


def _ssm(uts, seqs, lam_re, lam_im, log_dt, b_re, b_im, c_re, c_im, d_skip):
    width = uts[0].shape[1]
    _, groups, P = lam_re.shape
    C = SSM_GROUP
    assert P == SSM_STATE and b_re.shape[-1] == C and groups * C == width
    assert all(s % TOK_BLOCK == 0 for _, s in seqs)
    assert all(ut.shape == (b * s // TOK_BLOCK, width, TOK_BLOCK) for ut, (b, s) in zip(uts, seqs))
    nblk = sum(ut.shape[0] for ut in uts)

    def lanes_fb(a):
        return jnp.transpose(a, (1, 0, 2)).reshape(groups, 2 * P)

    ldt = jnp.broadcast_to(log_dt[:, :, None], (2, groups, P))
    rowp = jnp.stack([lanes_fb(lam_re), lanes_fb(lam_im), lanes_fb(ldt)], axis=1)
    colp = jnp.transpose(rowp, (0, 2, 1))
    bt = lambda b: jnp.transpose(b, (1, 3, 0, 2)).reshape(groups, C, 2 * P)
    cr = lambda c: jnp.transpose(c, (1, 2, 0, 3)).reshape(groups, C, 2 * P)
    ct = lambda c: jnp.transpose(c, (1, 0, 3, 2)).reshape(groups, 2 * P, C)
    dcol = jnp.tile(d_skip.reshape(groups, 1, C), (1, C, 1)).reshape(groups, C * C, 1)

    g3 = lambda s1, s2: pl.BlockSpec((1, s1, s2), lambda g: (g, 0, 0))
    acts = [pl.BlockSpec((ut.shape[0], C, TOK_BLOCK), lambda g: (0, g, 0)) for ut in uts]
    rows2 = 2 * nblk
    return pl.pallas_call(
        functools.partial(_ssm_kernel, seqs=tuple(seqs)),
        out_shape=[jax.ShapeDtypeStruct(ut.shape, BF16) for ut in uts],
        grid=(groups,),
        in_specs=acts + [g3(3, 2 * P), g3(2 * P, 3), g3(C, 2 * P), g3(C, 2 * P), g3(C, 2 * P), g3(C, 2 * P),
                         g3(2 * P, C), g3(2 * P, C), g3(C * C, 1)],
        out_specs=acts,
        scratch_shapes=[
            pltpu.VMEM((CHUNK_W, CHUNK_W), BF16),
            pltpu.VMEM((CHUNK_W, 4 * P), BF16),
            pltpu.VMEM((4 * P, CHUNK_W), BF16),
            pltpu.VMEM((C * C, LANES), F32),
            pltpu.VMEM((nblk * C, TOK_BLOCK), F32),
            pltpu.VMEM((nblk * C, TOK_BLOCK), F32),
        ] + [pltpu.VMEM((rows2, 2 * P), F32)] * 6,
        compiler_params=_cparams(("parallel",), 40 << 20),
        name="ssm",
    )(*uts, rowp, colp, bt(b_re), bt(b_im), cr(c_re), cr(c_im), ct(c_re), ct(c_im), dcol)


def _merge_kernel(x_ref, a_ref, yt_ref, gate_ref, wglu_ref, bglu_ref, wao_ref, wso_ref, wout_ref, o_ref):
    d = x_ref.shape[1]
    y = jnp.concatenate([yt_ref[j].astype(F32).T for j in range(yt_ref.shape[0])], axis=0)
    z = jax.nn.gelu(y)
    z = z * jax.nn.sigmoid(_dot(z.astype(BF16), wglu_ref[...]) + bglu_ref[...])
    m = _dot(z.astype(BF16), wso_ref[...])
    a = _dot(a_ref[...], wao_ref[...])
    g_attn, g_ssm = gate_ref[:, 0:d].astype(F32), gate_ref[:, d:2 * d].astype(F32)
    merged = jax.nn.sigmoid(g_attn) * a + jax.nn.sigmoid(g_ssm) * m
    o_ref[...] = x_ref[...] + _dot(merged.astype(BF16), wout_ref[...])


def _merge(x, attn, yt, gates, w_glu, b_glu, w_ao, w_so, w_out):
    t, d = x.shape
    aw, sw = attn.shape[1], yt.shape[1]
    tm = MERGE_ROWS
    weights = (sw * sw + aw * d + sw * d + d * d) * 2
    vmem = weights + 2 * tm * (2 * d * 4 + (aw + sw + 2 * d) * 2) + 6 * tm * d * 4 + (4 << 20)
    return pl.pallas_call(
        _merge_kernel,
        out_shape=jax.ShapeDtypeStruct((t, d), F32),
        grid=(t // tm,),
        in_specs=[
            pl.BlockSpec((tm, d), lambda i: (i, 0)),
            pl.BlockSpec((tm, aw), lambda i: (i, 0)),
            pl.BlockSpec((tm // TOK_BLOCK, sw, TOK_BLOCK), lambda i: (i, 0, 0)),
            pl.BlockSpec((tm, 2 * d), lambda i: (i, 0)),
            _const_spec((sw, sw)), _const_spec((1, sw)), _const_spec((aw, d)), _const_spec((sw, d)),
            _const_spec((d, d)),
        ],
        out_specs=pl.BlockSpec((tm, d), lambda i: (i, 0)),
        compiler_params=_cparams(("parallel",), vmem),
        name="mixer_out",
    )(x, attn, yt, gates, w_glu, b_glu.reshape(1, sw), w_ao, w_so, w_out)


def _layer(xs, ffn1_norm, ffn1_w_gate_up, ffn1_w_down, mix_norm, w_in, q_norm, k_norm, attn_sink,
           ssm_lambda_re, ssm_lambda_im, ssm_log_dt, ssm_b_re, ssm_b_im, ssm_c_re, ssm_c_im, ssm_d,
           w_glu, b_glu, w_attn_out, w_ssm_out, w_out, ffn2_norm, ffn2_w_gate_up, ffn2_w_down):
    d = xs[0].shape[-1]
    seqs = [(x.shape[0], x.shape[1]) for x in xs]
    aw = N_Q_HEADS * HEAD_DIM
    kw = N_KV_HEADS * HEAD_DIM
    sw = ssm_d.shape[0]

    o_u, o_g = aw + 2 * kw, aw + 2 * kw + sw
    w_qkv = w_in[:, :o_u].astype(BF16)
    w_u_t = w_in[:, o_u:o_g].T.astype(BF16)
    w_gates = w_in[:, o_g:].astype(BF16)
    w1_gu, w1_d = _ffn_weights(ffn1_w_gate_up, ffn1_w_down)
    w2_gu, w2_d = _ffn_weights(ffn2_w_gate_up, ffn2_w_down)
    w_glu_b, w_ao, w_so, w_o = (w.astype(BF16) for w in (w_glu, w_attn_out, w_ssm_out, w_out))
    cos2, sin2 = _rope_tables(max(s for _, s in seqs))

    x1s, attns, gates, uts = [], [], [], []
    for x, (b, s) in zip(xs, seqs):
        x1 = _ffn(x.reshape(b * s, d), ffn1_norm, w1_gu, w1_d)
        tm = math.gcd(INPROJ_ROWS, s)
        qkv = _inproj_qkv(x1, mix_norm, w_qkv, tm, s, cos2, sin2, q_norm, k_norm)
        uts.append(_inproj_u(x1, mix_norm, w_u_t, tm))
        gates.append(_inproj_gates(x1, mix_norm, w_gates, tm))
        attns.append(_attention(qkv, b, s, attn_sink))
        x1s.append(x1)

    yts = _ssm(uts, seqs, ssm_lambda_re, ssm_lambda_im, ssm_log_dt, ssm_b_re, ssm_b_im, ssm_c_re, ssm_c_im, ssm_d)

    outs = []
    for x_in, x1, attn, yt, gate in zip(xs, x1s, attns, yts, gates):
        x2 = _merge(x1, attn, yt, gate, w_glu_b, b_glu, w_ao, w_so, w_o)
        outs.append(_ffn(x2, ffn2_norm, w2_gu, w2_d).reshape(x_in.shape))
    return tuple(outs)


def kernel(x_prompt, x_sample, ffn1_norm, ffn1_w_gate_up, ffn1_w_down, mix_norm, w_in, q_norm, k_norm, attn_sink, ssm_lambda_re, ssm_lambda_im, ssm_log_dt, ssm_b_re, ssm_b_im, ssm_c_re, ssm_c_im, ssm_d, w_glu, b_glu, w_attn_out, w_ssm_out, w_out, ffn2_norm, ffn2_w_gate_up, ffn2_w_down):
    xs = (x_prompt, x_sample)
    depth = ffn1_norm.shape[0]
    params = (ffn1_norm, ffn1_w_gate_up, ffn1_w_down, mix_norm, w_in, q_norm, k_norm, attn_sink,
              ssm_lambda_re, ssm_lambda_im, ssm_log_dt, ssm_b_re, ssm_b_im, ssm_c_re, ssm_c_im, ssm_d,
              w_glu, b_glu, w_attn_out, w_ssm_out, w_out, ffn2_norm, ffn2_w_gate_up, ffn2_w_down)
    for l in range(depth):
        xs = _layer(xs, *(p[l] for p in params))
    return xs
```

```python
import functools
import math

import jax
import jax.numpy as jnp
import numpy as np
from jax import lax
from jax.experimental import pallas as pl
from jax.experimental.pallas import tpu as pltpu

F32 = jnp.float32
BF16 = jnp.bfloat16

HEAD_DIM = 128
N_Q_HEADS = 8
N_KV_HEADS = 2
Q_PER_KV = N_Q_HEADS // N_KV_HEADS
WINDOW = 128
ATTN_BLOCK = 128
ROPE_THETA = 10000.0
SSM_GROUP = 16
SSM_STATE = 64
EPS = 1e-6
LOG2E = math.log2(math.e)

LANES = 128
VMEM_LIMIT_CAP = 56 * 1024 * 1024

SSM_CHUNK = 64
CHUNK_W = SSM_CHUNK * SSM_GROUP
TOK_BLOCK = 128
FFN_ROWS = 512
FFN_COLS = 512
INPROJ_ROWS = 256
MERGE_ROWS = 256

assert 2 * SSM_CHUNK == TOK_BLOCK == LANES and 2 * SSM_STATE == LANES


def _cparams(semantics, vmem_bytes):
    return pltpu.CompilerParams(dimension_semantics=semantics,
                                vmem_limit_bytes=min(int(vmem_bytes), VMEM_LIMIT_CAP))


def _rms(x, gain):
    return x * lax.rsqrt(jnp.mean(x * x, axis=-1, keepdims=True) + EPS) * gain


def _dot(a, b):
    return jnp.dot(a, b, preferred_element_type=F32)


def _dot_exact(a, b):
    return jnp.dot(a, b, preferred_element_type=F32, precision=lax.Precision.HIGHEST)


def _ffn_kernel(x_ref, gain_ref, wg_ref, wu_ref, wd_ref, o_ref, h_ref):
    @pl.when(pl.program_id(1) == 0)
    def _():
        x = x_ref[...]
        h_ref[...] = _rms(x, gain_ref[...]).astype(BF16)
        o_ref[...] = x

    h = h_ref[...]
    g = _dot(h, wg_ref[...])
    u = _dot(h, wu_ref[...])
    a = (g * jax.nn.sigmoid(g)) * u
    o_ref[...] += _dot(a.astype(BF16), wd_ref[...])


def _ffn_weights(w_gate_up, w_down):
    return w_gate_up.astype(BF16), (0.5 * w_down).astype(BF16)


def _ffn(x, gain, w_gate_up, w_down_half):
    t, d = x.shape
    d_ff = w_down_half.shape[0]
    tm, tf = FFN_ROWS, FFN_COLS
    nf = d_ff // tf
    vmem = (4 * tm * d * 4) + (tm * d * 2) + 2 * 3 * (d * tf * 2) + 4 * (tm * tf * 4) + (4 << 20)
    return pl.pallas_call(
        _ffn_kernel,
        out_shape=jax.ShapeDtypeStruct((t, d), F32),
        grid=(t // tm, nf),
        in_specs=[
            pl.BlockSpec((tm, d), lambda i, f: (i, 0)),
            pl.BlockSpec((1, d), lambda i, f: (0, 0)),
            pl.BlockSpec((d, tf), lambda i, f: (0, f)),
            pl.BlockSpec((d, tf), lambda i, f: (0, f + nf)),
            pl.BlockSpec((tf, d), lambda i, f: (f, 0)),
        ],
        out_specs=pl.BlockSpec((tm, d), lambda i, f: (i, 0)),
        scratch_shapes=[pltpu.VMEM((tm, d), BF16)],
        compiler_params=_cparams(("parallel", "arbitrary"), vmem),
        name="ffn",
    )(x, gain.reshape(1, d), w_gate_up, w_gate_up, w_down_half)


def _rope(x, cos2, sin2):
    return x * cos2 + pltpu.roll(x, HEAD_DIM // 2, 1) * sin2


def _inproj_kernel(x_ref, gain_ref, wqkv_ref, wut_ref, wg_ref, cos_ref, sin_ref, qg_ref, kg_ref,
                   qkv_ref, ut_ref, gates_ref):
    h = _rms(x_ref[...], gain_ref[...]).astype(BF16)

    r = _dot(h, wqkv_ref[...])
    cos2, sin2 = cos_ref[...], sin_ref[...]
    qg = qg_ref[...] * (LOG2E * HEAD_DIM ** -0.5)
    kg = kg_ref[...]
    n_rot = N_Q_HEADS + N_KV_HEADS
    for head in range(n_rot):
        cols = slice(head * HEAD_DIM, (head + 1) * HEAD_DIM)
        gain = qg if head < N_Q_HEADS else kg
        qkv_ref[:, cols] = _rope(_rms(r[:, cols], gain), cos2, sin2).astype(BF16)
    vcols = slice(n_rot * HEAD_DIM, (n_rot + N_KV_HEADS) * HEAD_DIM)
    qkv_ref[:, vcols] = r[:, vcols].astype(BF16)

    rt = lax.dot_general(wut_ref[...], h, (((1,), (1,)), ((), ())), preferred_element_type=F32)
    for j in range(ut_ref.shape[0]):
        ut_ref[j] = rt[:, j * TOK_BLOCK:(j + 1) * TOK_BLOCK].astype(BF16)

    gates_ref[...] = _dot(h, wg_ref[...]).astype(BF16)


def _const_spec(shape):
    return pl.BlockSpec(shape, lambda *_: (0,) * len(shape), pipeline_mode=pl.Buffered(1))


def _inproj(x, gain, w_qkv, w_u_t, w_gates, seq, cos2, sin2, q_gain, k_gain):
    t, d = x.shape
    nqkv, nu, ng = w_qkv.shape[1], w_u_t.shape[0], w_gates.shape[1]
    tm = math.gcd(INPROJ_ROWS, seq)
    tiles_per_seq = seq // tm
    n_all = nqkv + nu + ng
    vmem = d * n_all * 2 + 2 * tm * d * 4 + 2 * tm * n_all * 2 + tm * n_all * 4 + tm * d * 6 + (6 << 20)
    row = lambda n: pl.BlockSpec((tm, n), lambda i: (i, 0))
    table = pl.BlockSpec((tm, HEAD_DIM), lambda i: (i % tiles_per_seq, 0))
    return pl.pallas_call(
        _inproj_kernel,
        out_shape=(jax.ShapeDtypeStruct((t, nqkv), BF16),
                   jax.ShapeDtypeStruct((t // TOK_BLOCK, nu, TOK_BLOCK), BF16),
                   jax.ShapeDtypeStruct((t, ng), BF16)),
        grid=(t // tm,),
        in_specs=[row(d), _const_spec((1, d)), _const_spec((d, nqkv)), _const_spec((nu, d)), _const_spec((d, ng)),
                  table, table, _const_spec((1, HEAD_DIM)), _const_spec((1, HEAD_DIM))],
        out_specs=(row(nqkv), pl.BlockSpec((tm // TOK_BLOCK, nu, TOK_BLOCK), lambda i: (i, 0, 0)), row(ng)),
        compiler_params=_cparams(("parallel",), vmem),
        name="inproj",
    )(x, gain.reshape(1, d), w_qkv, w_u_t, w_gates, cos2, sin2,
      q_gain.reshape(1, HEAD_DIM), k_gain.reshape(1, HEAD_DIM))


def _rope_table_kernel(freq_ref, cos_ref, sin_ref):
    rows = cos_ref.shape[0]
    pos = (lax.broadcasted_iota(jnp.int32, (rows, HEAD_DIM), 0) + pl.program_id(0) * rows).astype(F32)
    ang = pos * freq_ref[...]
    lane = lax.broadcasted_iota(jnp.int32, (rows, HEAD_DIM), 1)
    cos_ref[...] = jnp.cos(ang)
    sin_ref[...] = jnp.where(lane < HEAD_DIM // 2, -1.0, 1.0) * jnp.sin(ang)


def _rope_tables(seq):
    inv_freq = ROPE_THETA ** (-jnp.arange(0, HEAD_DIM, 2, dtype=F32) / HEAD_DIM)
    freq2 = jnp.concatenate([inv_freq, inv_freq]).reshape(1, HEAD_DIM)
    rows = math.gcd(seq, 512)
    return pl.pallas_call(
        _rope_table_kernel,
        out_shape=(jax.ShapeDtypeStruct((seq, HEAD_DIM), F32),) * 2,
        grid=(seq // rows,),
        in_specs=[pl.BlockSpec((1, HEAD_DIM), lambda i: (0, 0))],
        out_specs=(pl.BlockSpec((rows, HEAD_DIM), lambda i: (i, 0)),) * 2,
        compiler_params=_cparams(("parallel",), 16 << 20),
        name="rope_tables",
    )(freq2)


def _attn_bias(nq):
    blk = ATTN_BLOCK
    kj = np.arange(3 * blk)[:, None]
    qi = np.arange(nq)[None, :] % blk
    ok = np.abs(kj - blk - qi) <= WINDOW
    variants = []
    for v in range(4):
        valid = ok & ((kj >= blk) | (v & 1 == 0)) & ((kj < 2 * blk) | (v & 2 == 0))
        variants.append(np.where(valid, 0.0, -np.inf))
    return jnp.asarray(np.stack(variants), F32)


def _attn_kernel(sink_ref, bias_ref, q_ref, kvp_ref, kvc_ref, kvn_ref, o_ref):
    n = pl.program_id(1)
    blk = ATTN_BLOCK
    kw = N_KV_HEADS * HEAD_DIM
    biases = (bias_ref[jnp.where(n == 0, 1, 0)], bias_ref[jnp.where(n == pl.num_programs(1) - 1, 2, 0)])

    for h in range(N_KV_HEADS):
        kc = slice(h * HEAD_DIM, (h + 1) * HEAD_DIM)
        vc = slice(kw + h * HEAD_DIM, kw + (h + 1) * HEAD_DIM)
        k4 = (kvp_ref[:, kc], kvc_ref[0:blk, kc], kvc_ref[blk:2 * blk, kc], kvn_ref[:, kc])
        v4 = (kvp_ref[:, vc], kvc_ref[0:blk, vc], kvc_ref[blk:2 * blk, vc], kvn_ref[:, vc])
        heads = range(h * Q_PER_KV, (h + 1) * Q_PER_KV)
        sink = jnp.concatenate([jnp.full((1, blk), sink_ref[hd] * LOG2E, F32) for hd in heads], axis=1)
        for qb in range(2):
            rows = slice(qb * blk, (qb + 1) * blk)
            k_band = jnp.concatenate(k4[qb:qb + 3], axis=0)
            v_band = jnp.concatenate(v4[qb:qb + 3], axis=0)
            q_all = jnp.concatenate([q_ref[rows, hd * HEAD_DIM:(hd + 1) * HEAD_DIM] for hd in heads], axis=0)
            s = lax.dot_general(k_band, q_all, (((1,), (1,)), ((), ())), preferred_element_type=F32) + biases[qb]
            m = jnp.maximum(jnp.max(s, axis=0, keepdims=True), sink)
            p = jnp.exp2(s - m)
            denom = jnp.sum(p, axis=0, keepdims=True) + jnp.exp2(sink - m)
            ot = lax.dot_general(v_band, p.astype(BF16), (((0,), (0,)), ((), ())), preferred_element_type=F32)
            ot = ot / denom
            for g, hd in enumerate(heads):
                o_ref[rows, hd * HEAD_DIM:(hd + 1) * HEAD_DIM] = ot[:, g * blk:(g + 1) * blk].T.astype(BF16)


def _attention(qkv, batch, seq, sink):
    blk = ATTN_BLOCK
    assert seq % (2 * blk) == 0
    nb = seq // (2 * blk)
    qw = N_Q_HEADS * HEAD_DIM
    kvw = 2 * N_KV_HEADS * HEAD_DIM
    assert qw % kvw == 0
    kv_col = qw // kvw
    nq = Q_PER_KV * blk
    grid_spec = pltpu.PrefetchScalarGridSpec(
        num_scalar_prefetch=1,
        grid=(batch, nb),
        in_specs=[
            _const_spec((4, 3 * blk, nq)),
            pl.BlockSpec((2 * blk, qw), lambda b, n, s: (b * nb + n, 0)),
            pl.BlockSpec((blk, kvw), lambda b, n, s: (2 * (b * nb) + jnp.maximum(2 * n - 1, 0), kv_col)),
            pl.BlockSpec((2 * blk, kvw), lambda b, n, s: (b * nb + n, kv_col)),
            pl.BlockSpec((blk, kvw), lambda b, n, s: (2 * (b * nb) + jnp.minimum(2 * n + 2, 2 * nb - 1), kv_col)),
        ],
        out_specs=pl.BlockSpec((2 * blk, qw), lambda b, n, s: (b * nb + n, 0)),
    )
    return pl.pallas_call(
        _attn_kernel,
        out_shape=jax.ShapeDtypeStruct((batch * seq, qw), BF16),
        grid_spec=grid_spec,
        compiler_params=_cparams(("parallel", "parallel"), 32 << 20),
        name="attention",
    )(sink, _attn_bias(nq), qkv, qkv, qkv, qkv)


def _cmul(ar, ai, br, bi):
    return ar * br - ai * bi, ar * bi + ai * br


def _cpow_table(br, bi, expo, nbits):
    pr = jnp.where((expo & 1) == 1, br, 1.0)
    pi = jnp.where((expo & 1) == 1, bi, 0.0)
    for k in range(1, nbits):
        br, bi = _cmul(br, bi, br, bi)
        bit = ((expo >> k) & 1) == 1
        nr, ni = _cmul(pr, pi, br, bi)
        pr = jnp.where(bit, nr, pr)
        pi = jnp.where(bit, ni, pi)
    return pr, pi


def _lambda_bar(lre, lim, log_dt):
    dt = jnp.exp(log_dt)
    mag = jnp.exp(lre * dt)
    return mag * jnp.cos(lim * dt), mag * jnp.sin(lim * dt)


def _ssm_weights(rowp_ref, colp_ref, btr_ref, bti_ref, crr_ref, cri_ref, ctr_ref, cti_ref, dcol_ref,
                 t_s, wz_s, wy_s, g_s):
    L, P, C = SSM_CHUNK, SSM_STATE, SSM_GROUP
    nbits = L.bit_length()

    lre, lim, ldt = rowp_ref[0, 0:1, :], rowp_ref[0, 1:2, :], rowp_ref[0, 2:3, :]
    lbr, lbi = _lambda_bar(lre, lim, ldt)
    den = lre * lre + lim * lim
    cfr = ((lbr - 1.0) * lre + lbi * lim) / den
    cfi = (lbi * lre - (lbr - 1.0) * lim) / den
    bbr, bbi = _cmul(cfr, cfi, btr_ref[0], bti_ref[0])

    dr, di = lbr, lbi
    for _ in range(L.bit_length() - 1):
        dr, di = _cmul(dr, di, dr, di)

    n_sub = lax.broadcasted_iota(jnp.int32, (L, 2 * P), 0)
    lane = lax.broadcasted_iota(jnp.int32, (L, 2 * P), 1)
    pzr, pzi = _cpow_table(lbr, lbi, jnp.where(lane < P, L - 1 - n_sub, n_sub), nbits)
    for c in range(C):
        zr, zi = _cmul(pzr, pzi, bbr[c:c + 1], bbi[c:c + 1])
        wz_s[c * L:(c + 1) * L, 0:2 * P] = zr.astype(BF16)
        wz_s[c * L:(c + 1) * L, 2 * P:4 * P] = zi.astype(BF16)

    b_rep_r = jnp.concatenate([jnp.broadcast_to(bbr[c:c + 1], (C, 2 * P)) for c in range(C)], axis=0)
    b_rep_i = jnp.concatenate([jnp.broadcast_to(bbi[c:c + 1], (C, 2 * P)) for c in range(C)], axis=0)
    c_tile_r = jnp.concatenate([crr_ref[0]] * C, axis=0)
    c_tile_i = jnp.concatenate([cri_ref[0]] * C, axis=0)
    bcr, bci = _cmul(b_rep_r, b_rep_i, c_tile_r, c_tile_i)

    clre, clim, cldt = colp_ref[0, :, 0:1], colp_ref[0, :, 1:2], colp_ref[0, :, 2:3]
    cbr, cbi = _lambda_bar(clre, clim, cldt)
    sub = lax.broadcasted_iota(jnp.int32, (2 * P, LANES), 0)
    m_lane = lax.broadcasted_iota(jnp.int32, (2 * P, LANES), 1)

    pgr, pgi = _cpow_table(cbr, cbi, jnp.abs(m_lane - (L - 1)), nbits)
    live = ((sub < P) & (m_lane >= L - 1)) | ((sub >= P) & (m_lane <= L - 1))
    pgr = jnp.where(live, pgr, 0.0)
    pgi = jnp.where(live, pgi, 0.0)
    g = _dot_exact(bcr, pgr) - _dot_exact(bci, pgi)
    g_row = lax.broadcasted_iota(jnp.int32, (C * C, LANES), 0)
    g_lane = lax.broadcasted_iota(jnp.int32, (C * C, LANES), 1)
    log2c = C.bit_length() - 1
    skip = ((g_row >> log2c) == (g_row & (C - 1))) & (g_lane == L - 1)
    g_s[...] = g + jnp.where(skip, dcol_ref[0], 0.0)

    j_lane = m_lane & (L - 1)
    pyr, pyi = _cpow_table(cbr, cbi, jnp.where(sub < P, j_lane + 1, L - j_lane), nbits)
    ctr, cti = ctr_ref[0], cti_ref[0]
    low = m_lane < L
    for a in range(C // 2):
        cr = jnp.where(low, ctr[:, 2 * a:2 * a + 1], ctr[:, 2 * a + 1:2 * a + 2])
        ci = jnp.where(low, cti[:, 2 * a:2 * a + 1], cti[:, 2 * a + 1:2 * a + 2])
        yr, yi = _cmul(cr, ci, pyr, pyi)
        wy_s[0:2 * P, a * LANES:(a + 1) * LANES] = yr.astype(BF16)
        wy_s[2 * P:4 * P, a * LANES:(a + 1) * LANES] = (-yi).astype(BF16)

    t_low = lax.broadcasted_iota(jnp.int32, (L, LANES), 1) < L

    for c_in in range(C):
        for a in range(C // 2):
            row = c_in * C + 2 * a
            ge = jnp.broadcast_to(g_s[row:row + 1, :], (L, LANES))
            go = jnp.broadcast_to(g_s[row + 1:row + 2, :], (L, LANES))
            te = pltpu.roll(ge, L + 1, 1, stride=1, stride_axis=0)
            to = pltpu.roll(go, 1, 1, stride=1, stride_axis=0)
            t_s[c_in * L:(c_in + 1) * L, a * LANES:(a + 1) * LANES] = jnp.where(t_low, te, to).astype(BF16)
    return dr, di


def _ssm_kernel(*refs, seqs):
    P, L, C = SSM_STATE, SSM_CHUNK, SSM_GROUP
    ns = len(seqs)
    u_refs, refs = refs[:ns], refs[ns:]
    param_refs, refs = refs[:9], refs[9:]
    y_refs, refs = refs[:ns], refs[ns:]
    t_s, wz_s, wy_s, g_s, a_s, yt_s, zr_s, zi_s, fr_s, fi_s, rr_s, ri_s = refs
    blocks = [nseq * (slen // TOK_BLOCK) for nseq, slen in seqs]
    nblk = sum(blocks)
    ar, ai = _ssm_weights(*param_refs, t_s, wz_s, wy_s, g_s)

    blk0 = 0
    for u_ref, nb in zip(u_refs, blocks):
        a_s[blk0 * C:(blk0 + nb) * C, :] = u_ref[...].astype(F32).reshape(nb * C, TOK_BLOCK)
        blk0 += nb

    low = lax.broadcasted_iota(jnp.int32, (nblk, LANES), 1) < L
    first, second = [], []
    for a in range(C // 2):
        x0 = a_s[pl.ds(2 * a, nblk, stride=C), :]
        x1 = a_s[pl.ds(2 * a + 1, nblk, stride=C), :]
        first.append(jnp.where(low, x0, pltpu.roll(x1, L, 1)))
        second.append(jnp.where(low, pltpu.roll(x0, L, 1), x1))
    u = jnp.concatenate([jnp.concatenate(first, axis=1), jnp.concatenate(second, axis=1)], axis=0).astype(BF16)

    z = _dot(u, wz_s[...])
    zr_s[...] = z[:, 0:2 * P]
    zi_s[...] = z[:, 2 * P:4 * P]

    def scan(blk0, nseq, n):
        is_fwd = lax.broadcasted_iota(jnp.int32, (nseq, 2 * P), 1) < P

        def rows(block, second_half):
            return pl.ds(blk0 + block + (nblk if second_half else 0), nseq, stride=n)

        def half(carry, f_rows, b_rows):
            sr, si = carry
            fr_s[f_rows, :] = sr
            fi_s[f_rows, :] = si
            rr_s[b_rows, :] = sr
            ri_s[b_rows, :] = si
            zr = jnp.where(is_fwd, zr_s[f_rows, :], zr_s[b_rows, :])
            zi = jnp.where(is_fwd, zi_s[f_rows, :], zi_s[b_rows, :])
            return ar * sr - ai * si + zr, ar * si + ai * sr + zi

        def step(t, carry):
            carry = half(carry, rows(t, False), rows(n - 1 - t, True))
            return half(carry, rows(t, True), rows(n - 1 - t, False))

        zero = jnp.zeros((nseq, 2 * P), F32)
        lax.fori_loop(0, n, step, (zero, zero))

    blk0 = 0
    for (nseq, slen), nb in zip(seqs, blocks):
        scan(blk0, nseq, slen // TOK_BLOCK)
        blk0 += nb

    sel = lax.broadcasted_iota(jnp.int32, fr_s.shape, 1) < P
    s_in = jnp.concatenate([jnp.where(sel, fr_s[...], rr_s[...]),
                            jnp.where(sel, fi_s[...], ri_s[...])], axis=1).astype(BF16)
    y = _dot(u, t_s[...]) + _dot(s_in, wy_s[...])

    for a in range(C // 2):
        y0 = y[0:nblk, a * LANES:(a + 1) * LANES]
        y1 = y[nblk:2 * nblk, a * LANES:(a + 1) * LANES]
        yt_s[pl.ds(2 * a, nblk, stride=C), :] = jnp.where(low, y0, pltpu.roll(y1, L, 1))
        yt_s[pl.ds(2 * a + 1, nblk, stride=C), :] = jnp.where(low, pltpu.roll(y0, L, 1), y1)
    blk0 = 0
    for y_ref, nb in zip(y_refs, blocks):
        y_ref[...] = yt_s[blk0 * C:(blk0 + nb) * C, :].reshape(nb, C, TOK_BLOCK).astype(y_ref.dtype)
        blk0 += nb


def _ssm(uts, seqs, lam_re, lam_im, log_dt, b_re, b_im, c_re, c_im, d_skip):
    width = uts[0].shape[1]
    _, groups, P = lam_re.shape
    C = SSM_GROUP
    assert P == SSM_STATE and b_re.shape[-1] == C and groups * C == width
    assert all(s % TOK_BLOCK == 0 for _, s in seqs)
    assert all(ut.shape == (b * s // TOK_BLOCK, width, TOK_BLOCK) for ut, (b, s) in zip(uts, seqs))
    nblk = sum(ut.shape[0] for ut in uts)

    def lanes_fb(a):
        return jnp.transpose(a, (1, 0, 2)).reshape(groups, 2 * P)

    ldt = jnp.broadcast_to(log_dt[:, :, None], (2, groups, P))
    rowp = jnp.stack([lanes_fb(lam_re), lanes_fb(lam_im), lanes_fb(ldt)], axis=1)
    colp = jnp.transpose(rowp, (0, 2, 1))
    bt = lambda b: jnp.transpose(b, (1, 3, 0, 2)).reshape(groups, C, 2 * P)
    cr = lambda c: jnp.transpose(c, (1, 2, 0, 3)).reshape(groups, C, 2 * P)
    ct = lambda c: jnp.transpose(c, (1, 0, 3, 2)).reshape(groups, 2 * P, C)
    dcol = jnp.tile(d_skip.reshape(groups, 1, C), (1, C, 1)).reshape(groups, C * C, 1)

    g3 = lambda s1, s2: pl.BlockSpec((1, s1, s2), lambda g: (g, 0, 0))
    acts = [pl.BlockSpec((ut.shape[0], C, TOK_BLOCK), lambda g: (0, g, 0)) for ut in uts]
    rows2 = 2 * nblk
    return pl.pallas_call(
        functools.partial(_ssm_kernel, seqs=tuple(seqs)),
        out_shape=[jax.ShapeDtypeStruct(ut.shape, BF16) for ut in uts],
        grid=(groups,),
        in_specs=acts + [g3(3, 2 * P), g3(2 * P, 3), g3(C, 2 * P), g3(C, 2 * P), g3(C, 2 * P), g3(C, 2 * P),
                         g3(2 * P, C), g3(2 * P, C), g3(C * C, 1)],
        out_specs=acts,
        scratch_shapes=[
            pltpu.VMEM((CHUNK_W, CHUNK_W), BF16),
            pltpu.VMEM((CHUNK_W, 4 * P), BF16),
            pltpu.VMEM((4 * P, CHUNK_W), BF16),
            pltpu.VMEM((C * C, LANES), F32),
            pltpu.VMEM((nblk * C, TOK_BLOCK), F32),
            pltpu.VMEM((nblk * C, TOK_BLOCK), F32),
        ] + [pltpu.VMEM((rows2, 2 * P), F32)] * 6,
        compiler_params=_cparams(("parallel",), 40 << 20),
        name="ssm",
    )(*uts, rowp, colp, bt(b_re), bt(b_im), cr(c_re), cr(c_im), ct(c_re), ct(c_im), dcol)


def _merge_kernel(x_ref, a_ref, yt_ref, gate_ref, wglu_ref, bglu_ref, wao_ref, wso_ref, wout_ref, o_ref):
    d = x_ref.shape[1]
    y = jnp.concatenate([yt_ref[j].astype(F32).T for j in range(yt_ref.shape[0])], axis=0)
    z = jax.nn.gelu(y)
    z = z * jax.nn.sigmoid(_dot(z.astype(BF16), wglu_ref[...]) + bglu_ref[...])
    m = _dot(z.astype(BF16), wso_ref[...])
    a = _dot(a_ref[...], wao_ref[...])
    g_attn, g_ssm = gate_ref[:, 0:d].astype(F32), gate_ref[:, d:2 * d].astype(F32)
    merged = jax.nn.sigmoid(g_attn) * a + jax.nn.sigmoid(g_ssm) * m
    o_ref[...] = x_ref[...] + _dot(merged.astype(BF16), wout_ref[...])


def _merge(x, attn, yt, gates, w_glu, b_glu, w_ao, w_so, w_out):
    t, d = x.shape
    aw, sw = attn.shape[1], yt.shape[1]
    tm = MERGE_ROWS
    weights = (sw * sw + aw * d + sw * d + d * d) * 2
    vmem = weights + 2 * tm * (2 * d * 4 + (aw + sw + 2 * d) * 2) + 6 * tm * d * 4 + (4 << 20)
    return pl.pallas_call(
        _merge_kernel,
        out_shape=jax.ShapeDtypeStruct((t, d), F32),
        grid=(t // tm,),
        in_specs=[
            pl.BlockSpec((tm, d), lambda i: (i, 0)),
            pl.BlockSpec((tm, aw), lambda i: (i, 0)),
            pl.BlockSpec((tm // TOK_BLOCK, sw, TOK_BLOCK), lambda i: (i, 0, 0)),
            pl.BlockSpec((tm, 2 * d), lambda i: (i, 0)),
            _const_spec((sw, sw)), _const_spec((1, sw)), _const_spec((aw, d)), _const_spec((sw, d)),
            _const_spec((d, d)),
        ],
        out_specs=pl.BlockSpec((tm, d), lambda i: (i, 0)),
        compiler_params=_cparams(("parallel",), vmem),
        name="mixer_out",
    )(x, attn, yt, gates, w_glu, b_glu.reshape(1, sw), w_ao, w_so, w_out)


def _layer(xs, ffn1_norm, ffn1_w_gate_up, ffn1_w_down, mix_norm, w_in, q_norm, k_norm, attn_sink,
           ssm_lambda_re, ssm_lambda_im, ssm_log_dt, ssm_b_re, ssm_b_im, ssm_c_re, ssm_c_im, ssm_d,
           w_glu, b_glu, w_attn_out, w_ssm_out, w_out, ffn2_norm, ffn2_w_gate_up, ffn2_w_down):
    d = xs[0].shape[-1]
    seqs = [(x.shape[0], x.shape[1]) for x in xs]
    aw = N_Q_HEADS * HEAD_DIM
    kw = N_KV_HEADS * HEAD_DIM
    sw = ssm_d.shape[0]

    o_u, o_g = aw + 2 * kw, aw + 2 * kw + sw
    w_qkv = w_in[:, :o_u].astype(BF16)
    w_u_t = w_in[:, o_u:o_g].T.astype(BF16)
    w_gates = w_in[:, o_g:].astype(BF16)
    w1_gu, w1_d = _ffn_weights(ffn1_w_gate_up, ffn1_w_down)
    w2_gu, w2_d = _ffn_weights(ffn2_w_gate_up, ffn2_w_down)
    w_glu_b, w_ao, w_so, w_o = (w.astype(BF16) for w in (w_glu, w_attn_out, w_ssm_out, w_out))
    cos2, sin2 = _rope_tables(max(s for _, s in seqs))

    x1s, attns, gates, uts = [], [], [], []
    for x, (b, s) in zip(xs, seqs):
        x1 = _ffn(x.reshape(b * s, d), ffn1_norm, w1_gu, w1_d)
        qkv, ut, gate = _inproj(x1, mix_norm, w_qkv, w_u_t, w_gates, s, cos2, sin2, q_norm, k_norm)
        uts.append(ut)
        gates.append(gate)
        attns.append(_attention(qkv, b, s, attn_sink))
        x1s.append(x1)

    yts = _ssm(uts, seqs, ssm_lambda_re, ssm_lambda_im, ssm_log_dt, ssm_b_re, ssm_b_im, ssm_c_re, ssm_c_im, ssm_d)

    outs = []
    for x_in, x1, attn, yt, gate in zip(xs, x1s, attns, yts, gates):
        x2 = _merge(x1, attn, yt, gate, w_glu_b, b_glu, w_ao, w_so, w_o)
        outs.append(_ffn(x2, ffn2_norm, w2_gu, w2_d).reshape(x_in.shape))
    return tuple(outs)


def kernel(x_prompt, x_sample, ffn1_norm, ffn1_w_gate_up, ffn1_w_down, mix_norm, w_in, q_norm, k_norm, attn_sink, ssm_lambda_re, ssm_lambda_im, ssm_log_dt, ssm_b_re, ssm_b_im, ssm_c_re, ssm_c_im, ssm_d, w_glu, b_glu, w_attn_out, w_ssm_out, w_out, ffn2_norm, ffn2_w_gate_up, ffn2_w_down):
    xs = (x_prompt, x_sample)
    depth = ffn1_norm.shape[0]
    params = (ffn1_norm, ffn1_w_gate_up, ffn1_w_down, mix_norm, w_in, q_norm, k_norm, attn_sink,
              ssm_lambda_re, ssm_lambda_im, ssm_log_dt, ssm_b_re, ssm_b_im, ssm_c_re, ssm_c_im, ssm_d,
              w_glu, b_glu, w_attn_out, w_ssm_out, w_out, ffn2_norm, ffn2_w_gate_up, ffn2_w_down)
    for l in range(depth):
        xs = _layer(xs, *(p[l] for p in params))
    return xs
```

```python
import functools
import math

import jax
import jax.numpy as jnp
import numpy as np
from jax import lax
from jax.experimental import pallas as pl
from jax.experimental.pallas import tpu as pltpu

F32 = jnp.float32
BF16 = jnp.bfloat16

HEAD_DIM = 128
N_Q_HEADS = 8
N_KV_HEADS = 2
Q_PER_KV = N_Q_HEADS // N_KV_HEADS
WINDOW = 128
ATTN_BLOCK = 128
ROPE_THETA = 10000.0
SSM_GROUP = 16
SSM_STATE = 64
EPS = 1e-6
LOG2E = math.log2(math.e)

LANES = 128
VMEM_LIMIT_CAP = 56 * 1024 * 1024

SSM_CHUNK = 64
CHUNK_W = SSM_CHUNK * SSM_GROUP
TOK_BLOCK = 128
FFN_ROWS = 1024
FFN_COLS = 512
INPROJ_ROWS = 256
MERGE_ROWS = 256

assert 2 * SSM_CHUNK == TOK_BLOCK == LANES and 2 * SSM_STATE == LANES


def _cparams(semantics, vmem_bytes):
    return pltpu.CompilerParams(dimension_semantics=semantics,
                                vmem_limit_bytes=min(int(vmem_bytes), VMEM_LIMIT_CAP))


def _rms(x, gain):
    return x * lax.rsqrt(jnp.mean(x * x, axis=-1, keepdims=True) + EPS) * gain


def _dot(a, b):
    return jnp.dot(a, b, preferred_element_type=F32)


def _dot_exact(a, b):
    return jnp.dot(a, b, preferred_element_type=F32, precision=lax.Precision.HIGHEST)


def _ffn_kernel(x_hbm, gain_ref, wg_ref, wu_ref, wd_ref, o_ref, x_buf, h_ref, x_sem):
    i, f = pl.program_id(0), pl.program_id(1)
    tm = o_ref.shape[0]

    def x_copy(tile):
        return pltpu.make_async_copy(x_hbm.at[pl.ds(pl.multiple_of(tile * tm, tm), tm), :], x_buf, x_sem)

    @pl.when(jnp.logical_and(i == 0, f == 0))
    def _():
        x_copy(0).start()

    @pl.when(f == 0)
    def _():
        x_copy(i).wait()
        x = x_buf[...]
        h_ref[...] = _rms(x, gain_ref[...]).astype(BF16)
        o_ref[...] = x

    @pl.when(jnp.logical_and(f == pl.num_programs(1) - 1, i + 1 < pl.num_programs(0)))
    def _():
        x_copy(i + 1).start()

    h = h_ref[...]
    g = _dot(h, wg_ref[...])
    u = _dot(h, wu_ref[...])
    a = (g * jax.nn.sigmoid(g)) * u
    o_ref[...] += _dot(a.astype(BF16), wd_ref[...])


def _ffn_weights(w_gate_up, w_down):
    return w_gate_up.astype(BF16), (0.5 * w_down).astype(BF16)


def _ffn(x, gain, w_gate_up, w_down_half):
    t, d = x.shape
    d_ff = w_down_half.shape[0]
    tm, tf = FFN_ROWS, FFN_COLS
    nf = d_ff // tf
    vmem = (3 * tm * d * 4) + (tm * d * 2) + 2 * 3 * (d * tf * 2) + 4 * (tm * tf * 4) + (4 << 20)
    return pl.pallas_call(
        _ffn_kernel,
        out_shape=jax.ShapeDtypeStruct((t, d), F32),
        grid=(t // tm, nf),
        in_specs=[
            pl.BlockSpec(memory_space=pl.ANY),
            pl.BlockSpec((1, d), lambda i, f: (0, 0)),
            pl.BlockSpec((d, tf), lambda i, f: (0, f)),
            pl.BlockSpec((d, tf), lambda i, f: (0, f + nf)),
            pl.BlockSpec((tf, d), lambda i, f: (f, 0)),
        ],
        out_specs=pl.BlockSpec((tm, d), lambda i, f: (i, 0)),
        scratch_shapes=[pltpu.VMEM((tm, d), F32), pltpu.VMEM((tm, d), BF16), pltpu.SemaphoreType.DMA],
        compiler_params=_cparams(("arbitrary", "arbitrary"), vmem),
        name="ffn",
    )(x, gain.reshape(1, d), w_gate_up, w_gate_up, w_down_half)


def _rope(x, cos2, sin2):
    return x * cos2 + pltpu.roll(x, HEAD_DIM // 2, 1) * sin2


def _inproj_kernel(x_ref, gain_ref, wqkv_ref, wut_ref, wg_ref, cos_ref, sin_ref, qg_ref, kg_ref,
                   qkv_ref, ut_ref, gates_ref):
    h = _rms(x_ref[...], gain_ref[...]).astype(BF16)

    r = _dot(h, wqkv_ref[...])
    cos2, sin2 = cos_ref[...], sin_ref[...]
    qg = qg_ref[...] * (LOG2E * HEAD_DIM ** -0.5)
    kg = kg_ref[...]
    n_rot = N_Q_HEADS + N_KV_HEADS
    for head in range(n_rot):
        cols = slice(head * HEAD_DIM, (head + 1) * HEAD_DIM)
        gain = qg if head < N_Q_HEADS else kg
        qkv_ref[:, cols] = _rope(_rms(r[:, cols], gain), cos2, sin2).astype(BF16)
    vcols = slice(n_rot * HEAD_DIM, (n_rot + N_KV_HEADS) * HEAD_DIM)
    qkv_ref[:, vcols] = r[:, vcols].astype(BF16)

    rt = lax.dot_general(wut_ref[...], h, (((1,), (1,)), ((), ())), preferred_element_type=F32)
    for j in range(ut_ref.shape[0]):
        ut_ref[j] = rt[:, j * TOK_BLOCK:(j + 1) * TOK_BLOCK].astype(BF16)

    gates_ref[...] = _dot(h, wg_ref[...]).astype(BF16)


def _const_spec(shape):
    return pl.BlockSpec(shape, lambda *_: (0,) * len(shape), pipeline_mode=pl.Buffered(1))


def _inproj(x, gain, w_qkv, w_u_t, w_gates, seq, cos2, sin2, q_gain, k_gain):
    t, d = x.shape
    nqkv, nu, ng = w_qkv.shape[1], w_u_t.shape[0], w_gates.shape[1]
    tm = math.gcd(INPROJ_ROWS, seq)
    tiles_per_seq = seq // tm
    n_all = nqkv + nu + ng
    vmem = d * n_all * 2 + 2 * tm * d * 4 + 2 * tm * n_all * 2 + tm * n_all * 4 + tm * d * 6 + (6 << 20)
    row = lambda n: pl.BlockSpec((tm, n), lambda i: (i, 0))
    table = pl.BlockSpec((tm, HEAD_DIM), lambda i: (i % tiles_per_seq, 0))
    return pl.pallas_call(
        _inproj_kernel,
        out_shape=(jax.ShapeDtypeStruct((t, nqkv), BF16),
                   jax.ShapeDtypeStruct((t // TOK_BLOCK, nu, TOK_BLOCK), BF16),
                   jax.ShapeDtypeStruct((t, ng), BF16)),
        grid=(t // tm,),
        in_specs=[row(d), _const_spec((1, d)), _const_spec((d, nqkv)), _const_spec((nu, d)), _const_spec((d, ng)),
                  table, table, _const_spec((1, HEAD_DIM)), _const_spec((1, HEAD_DIM))],
        out_specs=(row(nqkv), pl.BlockSpec((tm // TOK_BLOCK, nu, TOK_BLOCK), lambda i: (i, 0, 0)), row(ng)),
        compiler_params=_cparams(("parallel",), vmem),
        name="inproj",
    )(x, gain.reshape(1, d), w_qkv, w_u_t, w_gates, cos2, sin2,
      q_gain.reshape(1, HEAD_DIM), k_gain.reshape(1, HEAD_DIM))


def _rope_table_kernel(freq_ref, cos_ref, sin_ref):
    rows = cos_ref.shape[0]
    pos = (lax.broadcasted_iota(jnp.int32, (rows, HEAD_DIM), 0) + pl.program_id(0) * rows).astype(F32)
    ang = pos * freq_ref[...]
    lane = lax.broadcasted_iota(jnp.int32, (rows, HEAD_DIM), 1)
    cos_ref[...] = jnp.cos(ang)
    sin_ref[...] = jnp.where(lane < HEAD_DIM // 2, -1.0, 1.0) * jnp.sin(ang)


def _rope_tables(seq):
    inv_freq = ROPE_THETA ** (-jnp.arange(0, HEAD_DIM, 2, dtype=F32) / HEAD_DIM)
    freq2 = jnp.concatenate([inv_freq, inv_freq]).reshape(1, HEAD_DIM)
    rows = math.gcd(seq, 512)
    return pl.pallas_call(
        _rope_table_kernel,
        out_shape=(jax.ShapeDtypeStruct((seq, HEAD_DIM), F32),) * 2,
        grid=(seq // rows,),
        in_specs=[pl.BlockSpec((1, HEAD_DIM), lambda i: (0, 0))],
        out_specs=(pl.BlockSpec((rows, HEAD_DIM), lambda i: (i, 0)),) * 2,
        compiler_params=_cparams(("parallel",), 16 << 20),
        name="rope_tables",
    )(freq2)


def _attn_bias(nq):
    blk = ATTN_BLOCK
    kj = np.arange(3 * blk)[:, None]
    qi = np.arange(nq)[None, :] % blk
    ok = np.abs(kj - blk - qi) <= WINDOW
    variants = []
    for v in range(4):
        valid = ok & ((kj >= blk) | (v & 1 == 0)) & ((kj < 2 * blk) | (v & 2 == 0))
        variants.append(np.where(valid, 0.0, -np.inf))
    return jnp.asarray(np.stack(variants), F32)


def _attn_kernel(sink_ref, bias_ref, q_ref, kvp_ref, kvc_ref, kvn_ref, o_ref):
    n = pl.program_id(1)
    blk = ATTN_BLOCK
    kw = N_KV_HEADS * HEAD_DIM
    biases = (bias_ref[jnp.where(n == 0, 1, 0)], bias_ref[jnp.where(n == pl.num_programs(1) - 1, 2, 0)])

    for h in range(N_KV_HEADS):
        kc = slice(h * HEAD_DIM, (h + 1) * HEAD_DIM)
        vc = slice(kw + h * HEAD_DIM, kw + (h + 1) * HEAD_DIM)
        k4 = (kvp_ref[:, kc], kvc_ref[0:blk, kc], kvc_ref[blk:2 * blk, kc], kvn_ref[:, kc])
        v4 = (kvp_ref[:, vc], kvc_ref[0:blk, vc], kvc_ref[blk:2 * blk, vc], kvn_ref[:, vc])
        heads = range(h * Q_PER_KV, (h + 1) * Q_PER_KV)
        sink = jnp.concatenate([jnp.full((1, blk), sink_ref[hd] * LOG2E, F32) for hd in heads], axis=1)
        for qb in range(2):
            rows = slice(qb * blk, (qb + 1) * blk)
            k_band = jnp.concatenate(k4[qb:qb + 3], axis=0)
            v_band = jnp.concatenate(v4[qb:qb + 3], axis=0)
            q_all = jnp.concatenate([q_ref[rows, hd * HEAD_DIM:(hd + 1) * HEAD_DIM] for hd in heads], axis=0)
            s = lax.dot_general(k_band, q_all, (((1,), (1,)), ((), ())), preferred_element_type=F32) + biases[qb]
            m = jnp.maximum(jnp.max(s, axis=0, keepdims=True), sink)
            p = jnp.exp2(s - m)
            denom = jnp.sum(p, axis=0, keepdims=True) + jnp.exp2(sink - m)
            ot = lax.dot_general(v_band, p.astype(BF16), (((0,), (0,)), ((), ())), preferred_element_type=F32)
            ot = ot / denom
            for g, hd in enumerate(heads):
                o_ref[rows, hd * HEAD_DIM:(hd + 1) * HEAD_DIM] = ot[:, g * blk:(g + 1) * blk].T.astype(BF16)


def _attention(qkv, batch, seq, sink):
    blk = ATTN_BLOCK
    assert seq % (2 * blk) == 0
    nb = seq // (2 * blk)
    qw = N_Q_HEADS * HEAD_DIM
    kvw = 2 * N_KV_HEADS * HEAD_DIM
    assert qw % kvw == 0
    kv_col = qw // kvw
    nq = Q_PER_KV * blk
    grid_spec = pltpu.PrefetchScalarGridSpec(
        num_scalar_prefetch=1,
        grid=(batch, nb),
        in_specs=[
            _const_spec((4, 3 * blk, nq)),
            pl.BlockSpec((2 * blk, qw), lambda b, n, s: (b * nb + n, 0)),
            pl.BlockSpec((blk, kvw), lambda b, n, s: (2 * (b * nb) + jnp.maximum(2 * n - 1, 0), kv_col)),
            pl.BlockSpec((2 * blk, kvw), lambda b, n, s: (b * nb + n, kv_col)),
            pl.BlockSpec((blk, kvw), lambda b, n, s: (2 * (b * nb) + jnp.minimum(2 * n + 2, 2 * nb - 1), kv_col)),
        ],
        out_specs=pl.BlockSpec((2 * blk, qw), lambda b, n, s: (b * nb + n, 0)),
    )
    return pl.pallas_call(
        _attn_kernel,
        out_shape=jax.ShapeDtypeStruct((batch * seq, qw), BF16),
        grid_spec=grid_spec,
        compiler_params=_cparams(("parallel", "parallel"), 32 << 20),
        name="attention",
    )(sink, _attn_bias(nq), qkv, qkv, qkv, qkv)


def _cmul(ar, ai, br, bi):
    return ar * br - ai * bi, ar * bi + ai * br


def _cpow_table(br, bi, expo, nbits):
    pr = jnp.where((expo & 1) == 1, br, 1.0)
    pi = jnp.where((expo & 1) == 1, bi, 0.0)
    for k in range(1, nbits):
        br, bi = _cmul(br, bi, br, bi)
        bit = ((expo >> k) & 1) == 1
        nr, ni = _cmul(pr, pi, br, bi)
        pr = jnp.where(bit, nr, pr)
        pi = jnp.where(bit, ni, pi)
    return pr, pi


def _lambda_bar(lre, lim, log_dt):
    dt = jnp.exp(log_dt)
    mag = jnp.exp(lre * dt)
    return mag * jnp.cos(lim * dt), mag * jnp.sin(lim * dt)


def _ssm_weights(rowp_ref, colp_ref, btr_ref, bti_ref, crr_ref, cri_ref, ctr_ref, cti_ref, dcol_ref,
                 t_s, wz_s, wy_s, g_s):
    L, P, C = SSM_CHUNK, SSM_STATE, SSM_GROUP
    nbits = L.bit_length()

    lre, lim, ldt = rowp_ref[0, 0:1, :], rowp_ref[0, 1:2, :], rowp_ref[0, 2:3, :]
    lbr, lbi = _lambda_bar(lre, lim, ldt)
    den = lre * lre + lim * lim
    cfr = ((lbr - 1.0) * lre + lbi * lim) / den
    cfi = (lbi * lre - (lbr - 1.0) * lim) / den
    bbr, bbi = _cmul(cfr, cfi, btr_ref[0], bti_ref[0])

    dr, di = lbr, lbi
    for _ in range(L.bit_length() - 1):
        dr, di = _cmul(dr, di, dr, di)

    n_sub = lax.broadcasted_iota(jnp.int32, (L, 2 * P), 0)
    lane = lax.broadcasted_iota(jnp.int32, (L, 2 * P), 1)
    pzr, pzi = _cpow_table(lbr, lbi, jnp.where(lane < P, L - 1 - n_sub, n_sub), nbits)
    for c in range(C):
        zr, zi = _cmul(pzr, pzi, bbr[c:c + 1], bbi[c:c + 1])
        wz_s[c * L:(c + 1) * L, 0:2 * P] = zr.astype(BF16)
        wz_s[c * L:(c + 1) * L, 2 * P:4 * P] = zi.astype(BF16)

    b_rep_r = jnp.concatenate([jnp.broadcast_to(bbr[c:c + 1], (C, 2 * P)) for c in range(C)], axis=0)
    b_rep_i = jnp.concatenate([jnp.broadcast_to(bbi[c:c + 1], (C, 2 * P)) for c in range(C)], axis=0)
    c_tile_r = jnp.concatenate([crr_ref[0]] * C, axis=0)
    c_tile_i = jnp.concatenate([cri_ref[0]] * C, axis=0)
    bcr, bci = _cmul(b_rep_r, b_rep_i, c_tile_r, c_tile_i)

    clre, clim, cldt = colp_ref[0, :, 0:1], colp_ref[0, :, 1:2], colp_ref[0, :, 2:3]
    cbr, cbi = _lambda_bar(clre, clim, cldt)
    sub = lax.broadcasted_iota(jnp.int32, (2 * P, LANES), 0)
    m_lane = lax.broadcasted_iota(jnp.int32, (2 * P, LANES), 1)

    pgr, pgi = _cpow_table(cbr, cbi, jnp.abs(m_lane - (L - 1)), nbits)
    live = ((sub < P) & (m_lane >= L - 1)) | ((sub >= P) & (m_lane <= L - 1))
    pgr = jnp.where(live, pgr, 0.0)
    pgi = jnp.where(live, pgi, 0.0)
    g = _dot_exact(bcr, pgr) - _dot_exact(bci, pgi)
    g_row = lax.broadcasted_iota(jnp.int32, (C * C, LANES), 0)
    g_lane = lax.broadcasted_iota(jnp.int32, (C * C, LANES), 1)
    log2c = C.bit_length() - 1
    skip = ((g_row >> log2c) == (g_row & (C - 1))) & (g_lane == L - 1)
    g_s[...] = g + jnp.where(skip, dcol_ref[0], 0.0)

    j_lane = m_lane & (L - 1)
    pyr, pyi = _cpow_table(cbr, cbi, jnp.where(sub < P, j_lane + 1, L - j_lane), nbits)
    ctr, cti = ctr_ref[0], cti_ref[0]
    low = m_lane < L
    for a in range(C // 2):
        cr = jnp.where(low, ctr[:, 2 * a:2 * a + 1], ctr[:, 2 * a + 1:2 * a + 2])
        ci = jnp.where(low, cti[:, 2 * a:2 * a + 1], cti[:, 2 * a + 1:2 * a + 2])
        yr, yi = _cmul(cr, ci, pyr, pyi)
        wy_s[0:2 * P, a * LANES:(a + 1) * LANES] = yr.astype(BF16)
        wy_s[2 * P:4 * P, a * LANES:(a + 1) * LANES] = (-yi).astype(BF16)

    t_low = lax.broadcasted_iota(jnp.int32, (L, LANES), 1) < L

    for c_in in range(C):
        for a in range(C // 2):
            row = c_in * C + 2 * a
            ge = jnp.broadcast_to(g_s[row:row + 1, :], (L, LANES))
            go = jnp.broadcast_to(g_s[row + 1:row + 2, :], (L, LANES))
            te = pltpu.roll(ge, L + 1, 1, stride=1, stride_axis=0)
            to = pltpu.roll(go, 1, 1, stride=1, stride_axis=0)
            t_s[c_in * L:(c_in + 1) * L, a * LANES:(a + 1) * LANES] = jnp.where(t_low, te, to).astype(BF16)
    return dr, di


def _ssm_kernel(*refs, seqs):
    P, L, C = SSM_STATE, SSM_CHUNK, SSM_GROUP
    ns = len(seqs)
    u_refs, refs = refs[:ns], refs[ns:]
    param_refs, refs = refs[:9], refs[9:]
    y_refs, refs = refs[:ns], refs[ns:]
    t_s, wz_s, wy_s, g_s, a_s, yt_s, zr_s, zi_s, fr_s, fi_s, rr_s, ri_s = refs
    blocks = [nseq * (slen // TOK_BLOCK) for nseq, slen in seqs]
    nblk = sum(blocks)
    ar, ai = _ssm_weights(*param_refs, t_s, wz_s, wy_s, g_s)

    blk0 = 0
    for u_ref, nb in zip(u_refs, blocks):
        a_s[blk0 * C:(blk0 + nb) * C, :] = u_ref[...].astype(F32).reshape(nb * C, TOK_BLOCK)
        blk0 += nb

    low = lax.broadcasted_iota(jnp.int32, (nblk, LANES), 1) < L
    first, second = [], []
    for a in range(C // 2):
        x0 = a_s[pl.ds(2 * a, nblk, stride=C), :]
        x1 = a_s[pl.ds(2 * a + 1, nblk, stride=C), :]
        first.append(jnp.where(low, x0, pltpu.roll(x1, L, 1)))
        second.append(jnp.where(low, pltpu.roll(x0, L, 1), x1))
    u = jnp.concatenate([jnp.concatenate(first, axis=1), jnp.concatenate(second, axis=1)], axis=0).astype(BF16)

    z = _dot(u, wz_s[...])
    zr_s[...] = z[:, 0:2 * P]
    zi_s[...] = z[:, 2 * P:4 * P]

    def scan(blk0, nseq, n):
        is_fwd = lax.broadcasted_iota(jnp.int32, (nseq, 2 * P), 1) < P

        def rows(block, second_half):
            return pl.ds(blk0 + block + (nblk if second_half else 0), nseq, stride=n)

        def half(carry, f_rows, b_rows):
            sr, si = carry
            fr_s[f_rows, :] = sr
            fi_s[f_rows, :] = si
            rr_s[b_rows, :] = sr
            ri_s[b_rows, :] = si
            zr = jnp.where(is_fwd, zr_s[f_rows, :], zr_s[b_rows, :])
            zi = jnp.where(is_fwd, zi_s[f_rows, :], zi_s[b_rows, :])
            return ar * sr - ai * si + zr, ar * si + ai * sr + zi

        def step(t, carry):
            carry = half(carry, rows(t, False), rows(n - 1 - t, True))
            return half(carry, rows(t, True), rows(n - 1 - t, False))

        zero = jnp.zeros((nseq, 2 * P), F32)
        lax.fori_loop(0, n, step, (zero, zero))

    blk0 = 0
    for (nseq, slen), nb in zip(seqs, blocks):
        scan(blk0, nseq, slen // TOK_BLOCK)
        blk0 += nb

    sel = lax.broadcasted_iota(jnp.int32, fr_s.shape, 1) < P
    s_in = jnp.concatenate([jnp.where(sel, fr_s[...], rr_s[...]),
                            jnp.where(sel, fi_s[...], ri_s[...])], axis=1).astype(BF16)
    y = _dot(u, t_s[...]) + _dot(s_in, wy_s[...])

    for a in range(C // 2):
        y0 = y[0:nblk, a * LANES:(a + 1) * LANES]
        y1 = y[nblk:2 * nblk, a * LANES:(a + 1) * LANES]
        yt_s[pl.ds(2 * a, nblk, stride=C), :] = jnp.where(low, y0, pltpu.roll(y1, L, 1))
        yt_s[pl.ds(2 * a + 1, nblk, stride=C), :] = jnp.where(low, pltpu.roll(y0, L, 1), y1)
    blk0 = 0
    for y_ref, nb in zip(y_refs, blocks):
        y_ref[...] = yt_s[blk0 * C:(blk0 + nb) * C, :].reshape(nb, C, TOK_BLOCK).astype(y_ref.dtype)
        blk0 += nb


def _ssm(uts, seqs, lam_re, lam_im, log_dt, b_re, b_im, c_re, c_im, d_skip):
    width = uts[0].shape[1]
    _, groups, P = lam_re.shape
    C = SSM_GROUP
    assert P == SSM_STATE and b_re.shape[-1] == C and groups * C == width
    assert all(s % TOK_BLOCK == 0 for _, s in seqs)
    assert all(ut.shape == (b * s // TOK_BLOCK, width, TOK_BLOCK) for ut, (b, s) in zip(uts, seqs))
    nblk = sum(ut.shape[0] for ut in uts)

    def lanes_fb(a):
        return jnp.transpose(a, (1, 0, 2)).reshape(groups, 2 * P)

    ldt = jnp.broadcast_to(log_dt[:, :, None], (2, groups, P))
    rowp = jnp.stack([lanes_fb(lam_re), lanes_fb(lam_im), lanes_fb(ldt)], axis=1)
    colp = jnp.transpose(rowp, (0, 2, 1))
    bt = lambda b: jnp.transpose(b, (1, 3, 0, 2)).reshape(groups, C, 2 * P)
    cr = lambda c: jnp.transpose(c, (1, 2, 0, 3)).reshape(groups, C, 2 * P)
    ct = lambda c: jnp.transpose(c, (1, 0, 3, 2)).reshape(groups, 2 * P, C)
    dcol = jnp.tile(d_skip.reshape(groups, 1, C), (1, C, 1)).reshape(groups, C * C, 1)

    g3 = lambda s1, s2: pl.BlockSpec((1, s1, s2), lambda g: (g, 0, 0))
    acts = [pl.BlockSpec((ut.shape[0], C, TOK_BLOCK), lambda g: (0, g, 0)) for ut in uts]
    rows2 = 2 * nblk
    return pl.pallas_call(
        functools.partial(_ssm_kernel, seqs=tuple(seqs)),
        out_shape=[jax.ShapeDtypeStruct(ut.shape, BF16) for ut in uts],
        grid=(groups,),
        in_specs=acts + [g3(3, 2 * P), g3(2 * P, 3), g3(C, 2 * P), g3(C, 2 * P), g3(C, 2 * P), g3(C, 2 * P),
                         g3(2 * P, C), g3(2 * P, C), g3(C * C, 1)],
        out_specs=acts,
        scratch_shapes=[
            pltpu.VMEM((CHUNK_W, CHUNK_W), BF16),
            pltpu.VMEM((CHUNK_W, 4 * P), BF16),
            pltpu.VMEM((4 * P, CHUNK_W), BF16),
            pltpu.VMEM((C * C, LANES), F32),
            pltpu.VMEM((nblk * C, TOK_BLOCK), F32),
            pltpu.VMEM((nblk * C, TOK_BLOCK), F32),
        ] + [pltpu.VMEM((rows2, 2 * P), F32)] * 6,
        compiler_params=_cparams(("parallel",), 40 << 20),
        name="ssm",
    )(*uts, rowp, colp, bt(b_re), bt(b_im), cr(c_re), cr(c_im), ct(c_re), ct(c_im), dcol)


def _merge_kernel(x_ref, a_ref, yt_ref, gate_ref, wglu_ref, bglu_ref, wao_ref, wso_ref, wout_ref, o_ref):
    d = x_ref.shape[1]
    y = jnp.concatenate([yt_ref[j].astype(F32).T for j in range(yt_ref.shape[0])], axis=0)
    z = jax.nn.gelu(y)
    z = z * jax.nn.sigmoid(_dot(z.astype(BF16), wglu_ref[...]) + bglu_ref[...])
    m = _dot(z.astype(BF16), wso_ref[...])
    a = _dot(a_ref[...], wao_ref[...])
    g_attn, g_ssm = gate_ref[:, 0:d].astype(F32), gate_ref[:, d:2 * d].astype(F32)
    merged = jax.nn.sigmoid(g_attn) * a + jax.nn.sigmoid(g_ssm) * m
    o_ref[...] = x_ref[...] + _dot(merged.astype(BF16), wout_ref[...])


def _merge(x, attn, yt, gates, w_glu, b_glu, w_ao, w_so, w_out):
    t, d = x.shape
    aw, sw = attn.shape[1], yt.shape[1]
    tm = MERGE_ROWS
    weights = (sw * sw + aw * d + sw * d + d * d) * 2
    vmem = weights + 2 * tm * (2 * d * 4 + (aw + sw + 2 * d) * 2) + 6 * tm * d * 4 + (4 << 20)
    return pl.pallas_call(
        _merge_kernel,
        out_shape=jax.ShapeDtypeStruct((t, d), F32),
        grid=(t // tm,),
        in_specs=[
            pl.BlockSpec((tm, d), lambda i: (i, 0)),
            pl.BlockSpec((tm, aw), lambda i: (i, 0)),
            pl.BlockSpec((tm // TOK_BLOCK, sw, TOK_BLOCK), lambda i: (i, 0, 0)),
            pl.BlockSpec((tm, 2 * d), lambda i: (i, 0)),
            _const_spec((sw, sw)), _const_spec((1, sw)), _const_spec((aw, d)), _const_spec((sw, d)),
            _const_spec((d, d)),
        ],
        out_specs=pl.BlockSpec((tm, d), lambda i: (i, 0)),
        compiler_params=_cparams(("parallel",), vmem),
        name="mixer_out",
    )(x, attn, yt, gates, w_glu, b_glu.reshape(1, sw), w_ao, w_so, w_out)


def _layer(xs, ffn1_norm, ffn1_w_gate_up, ffn1_w_down, mix_norm, w_in, q_norm, k_norm, attn_sink,
           ssm_lambda_re, ssm_lambda_im, ssm_log_dt, ssm_b_re, ssm_b_im, ssm_c_re, ssm_c_im, ssm_d,
           w_glu, b_glu, w_attn_out, w_ssm_out, w_out, ffn2_norm, ffn2_w_gate_up, ffn2_w_down):
    d = xs[0].shape[-1]
    seqs = [(x.shape[0], x.shape[1]) for x in xs]
    aw = N_Q_HEADS * HEAD_DIM
    kw = N_KV_HEADS * HEAD_DIM
    sw = ssm_d.shape[0]

    o_u, o_g = aw + 2 * kw, aw + 2 * kw + sw
    w_qkv = w_in[:, :o_u].astype(BF16)
    w_u_t = w_in[:, o_u:o_g].T.astype(BF16)
    w_gates = w_in[:, o_g:].astype(BF16)
    w1_gu, w1_d = _ffn_weights(ffn1_w_gate_up, ffn1_w_down)
    w2_gu, w2_d = _ffn_weights(ffn2_w_gate_up, ffn2_w_down)
    w_glu_b, w_ao, w_so, w_o = (w.astype(BF16) for w in (w_glu, w_attn_out, w_ssm_out, w_out))
    cos2, sin2 = _rope_tables(max(s for _, s in seqs))

    x1s, attns, gates, uts = [], [], [], []
    for x, (b, s) in zip(xs, seqs):
        x1 = _ffn(x.reshape(b * s, d), ffn1_norm, w1_gu, w1_d)
        qkv, ut, gate = _inproj(x1, mix_norm, w_qkv, w_u_t, w_gates, s, cos2, sin2, q_norm, k_norm)
        uts.append(ut)
        gates.append(gate)
        attns.append(_attention(qkv, b, s, attn_sink))
        x1s.append(x1)

    yts = _ssm(uts, seqs, ssm_lambda_re, ssm_lambda_im, ssm_log_dt, ssm_b_re, ssm_b_im, ssm_c_re, ssm_c_im, ssm_d)

    outs = []
    for x_in, x1, attn, yt, gate in zip(xs, x1s, attns, yts, gates):
        x2 = _merge(x1, attn, yt, gate, w_glu_b, b_glu, w_ao, w_so, w_o)
        outs.append(_ffn(x2, ffn2_norm, w2_gu, w2_d).reshape(x_in.shape))
    return tuple(outs)


def kernel(x_prompt, x_sample, ffn1_norm, ffn1_w_gate_up, ffn1_w_down, mix_norm, w_in, q_norm, k_norm, attn_sink, ssm_lambda_re, ssm_lambda_im, ssm_log_dt, ssm_b_re, ssm_b_im, ssm_c_re, ssm_c_im, ssm_d, w_glu, b_glu, w_attn_out, w_ssm_out, w_out, ffn2_norm, ffn2_w_gate_up, ffn2_w_down):
    xs = (x_prompt, x_sample)
    depth = ffn1_norm.shape[0]
    params = (ffn1_norm, ffn1_w_gate_up, ffn1_w_down, mix_norm, w_in, q_norm, k_norm, attn_sink,
              ssm_lambda_re, ssm_lambda_im, ssm_log_dt, ssm_b_re, ssm_b_im, ssm_c_re, ssm_c_im, ssm_d,
              w_glu, b_glu, w_attn_out, w_ssm_out, w_out, ffn2_norm, ffn2_w_gate_up, ffn2_w_down)
    for l in range(depth):
        xs = _layer(xs, *(p[l] for p in params))
    return xs
```

```python
import functools
import math

import jax
import jax.numpy as jnp
import numpy as np
from jax import lax
from jax.experimental import pallas as pl
from jax.experimental.pallas import tpu as pltpu

F32 = jnp.float32
BF16 = jnp.bfloat16

HEAD_DIM = 128
N_Q_HEADS = 8
N_KV_HEADS = 2
Q_PER_KV = N_Q_HEADS // N_KV_HEADS
WINDOW = 128
ATTN_BLOCK = 128
ROPE_THETA = 10000.0
SSM_GROUP = 16
SSM_STATE = 64
EPS = 1e-6
LOG2E = math.log2(math.e)

LANES = 128
VMEM_LIMIT_CAP = 56 * 1024 * 1024

SSM_CHUNK = 64
CHUNK_W = SSM_CHUNK * SSM_GROUP
TOK_BLOCK = 128
FFN_ROWS = 512
FFN_COLS = 512
INPROJ_ROWS = 256
MERGE_ROWS = 256

assert 2 * SSM_CHUNK == TOK_BLOCK == LANES and 2 * SSM_STATE == LANES


def _cparams(semantics, vmem_bytes):
    return pltpu.CompilerParams(dimension_semantics=semantics,
                                vmem_limit_bytes=min(int(vmem_bytes), VMEM_LIMIT_CAP))


def _rms(x, gain):
    return x * lax.rsqrt(jnp.mean(x * x, axis=-1, keepdims=True) + EPS) * gain


def _dot(a, b):
    return jnp.dot(a, b, preferred_element_type=F32)


def _dot_exact(a, b):
    return jnp.dot(a, b, preferred_element_type=F32, precision=lax.Precision.HIGHEST)


def _ffn_kernel(x_ref, gain_ref, wg_ref, wu_ref, wd_ref, o_ref, h_ref):
    @pl.when(pl.program_id(1) == 0)
    def _():
        x = x_ref[...]
        h_ref[...] = _rms(x, gain_ref[...]).astype(BF16)
        o_ref[...] = x

    h = h_ref[...]
    g = _dot(h, wg_ref[...])
    u = _dot(h, wu_ref[...])
    a = (g * jax.nn.sigmoid(g)) * u
    o_ref[...] += _dot(a.astype(BF16), wd_ref[...])


def _ffn_weights(w_gate_up, w_down):
    return w_gate_up.astype(BF16), (0.5 * w_down).astype(BF16)


def _ffn(x, gain, w_gate_up, w_down_half):
    t, d = x.shape
    d_ff = w_down_half.shape[0]
    tm, tf = FFN_ROWS, FFN_COLS
    nf = d_ff // tf
    vmem = (4 * tm * d * 4) + (tm * d * 2) + 2 * 3 * (d * tf * 2) + 4 * (tm * tf * 4) + (4 << 20)
    return pl.pallas_call(
        _ffn_kernel,
        out_shape=jax.ShapeDtypeStruct((t, d), F32),
        grid=(t // tm, nf),
        in_specs=[
            pl.BlockSpec((tm, d), lambda i, f: (i, 0)),
            pl.BlockSpec((1, d), lambda i, f: (0, 0)),
            pl.BlockSpec((d, tf), lambda i, f: (0, f)),
            pl.BlockSpec((d, tf), lambda i, f: (0, f + nf)),
            pl.BlockSpec((tf, d), lambda i, f: (f, 0)),
        ],
        out_specs=pl.BlockSpec((tm, d), lambda i, f: (i, 0)),
        scratch_shapes=[pltpu.VMEM((tm, d), BF16)],
        compiler_params=_cparams(("parallel", "arbitrary"), vmem),
        name="ffn",
    )(x, gain.reshape(1, d), w_gate_up, w_gate_up, w_down_half)


def _rope(x, cos2, sin2):
    return x * cos2 + pltpu.roll(x, HEAD_DIM // 2, 1) * sin2


def _inproj_kernel(x_ref, gain_ref, wqkv_ref, wut_ref, wg_ref, cos_ref, sin_ref, qg_ref, kg_ref,
                   qkv_ref, ut_ref, gates_ref):
    h = _rms(x_ref[...], gain_ref[...]).astype(BF16)

    r = _dot(h, wqkv_ref[...])
    cos2, sin2 = cos_ref[...], sin_ref[...]
    qg = qg_ref[...] * (LOG2E * HEAD_DIM ** -0.5)
    kg = kg_ref[...]
    n_rot = N_Q_HEADS + N_KV_HEADS
    for head in range(n_rot):
        cols = slice(head * HEAD_DIM, (head + 1) * HEAD_DIM)
        gain = qg if head < N_Q_HEADS else kg
        qkv_ref[:, cols] = _rope(_rms(r[:, cols], gain), cos2, sin2).astype(BF16)
    vcols = slice(n_rot * HEAD_DIM, (n_rot + N_KV_HEADS) * HEAD_DIM)
    qkv_ref[:, vcols] = r[:, vcols].astype(BF16)

    rt = lax.dot_general(wut_ref[...], h, (((1,), (1,)), ((), ())), preferred_element_type=F32)
    for j in range(ut_ref.shape[0]):
        ut_ref[j] = rt[:, j * TOK_BLOCK:(j + 1) * TOK_BLOCK].astype(BF16)

    gates_ref[...] = _dot(h, wg_ref[...]).astype(BF16)


def _const_spec(shape):
    return pl.BlockSpec(shape, lambda *_: (0,) * len(shape), pipeline_mode=pl.Buffered(1))


def _inproj(x, gain, w_qkv, w_u_t, w_gates, seq, cos2, sin2, q_gain, k_gain):
    t, d = x.shape
    nqkv, nu, ng = w_qkv.shape[1], w_u_t.shape[0], w_gates.shape[1]
    tm = math.gcd(INPROJ_ROWS, seq)
    tiles_per_seq = seq // tm
    n_all = nqkv + nu + ng
    vmem = d * n_all * 2 + 2 * tm * d * 4 + 2 * tm * n_all * 2 + tm * n_all * 4 + tm * d * 6 + (6 << 20)
    row = lambda n: pl.BlockSpec((tm, n), lambda i: (i, 0))
    table = pl.BlockSpec((tm, HEAD_DIM), lambda i: (i % tiles_per_seq, 0))
    return pl.pallas_call(
        _inproj_kernel,
        out_shape=(jax.ShapeDtypeStruct((t, nqkv), BF16),
                   jax.ShapeDtypeStruct((t // TOK_BLOCK, nu, TOK_BLOCK), BF16),
                   jax.ShapeDtypeStruct((t, ng), BF16)),
        grid=(t // tm,),
        in_specs=[row(d), _const_spec((1, d)), _const_spec((d, nqkv)), _const_spec((nu, d)), _const_spec((d, ng)),
                  table, table, _const_spec((1, HEAD_DIM)), _const_spec((1, HEAD_DIM))],
        out_specs=(row(nqkv), pl.BlockSpec((tm // TOK_BLOCK, nu, TOK_BLOCK), lambda i: (i, 0, 0)), row(ng)),
        compiler_params=_cparams(("parallel",), vmem),
        name="inproj",
    )(x, gain.reshape(1, d), w_qkv, w_u_t, w_gates, cos2, sin2,
      q_gain.reshape(1, HEAD_DIM), k_gain.reshape(1, HEAD_DIM))


def _rope_table_kernel(freq_ref, cos_ref, sin_ref):
    rows = cos_ref.shape[0]
    pos = (lax.broadcasted_iota(jnp.int32, (rows, HEAD_DIM), 0) + pl.program_id(0) * rows).astype(F32)
    ang = pos * freq_ref[...]
    lane = lax.broadcasted_iota(jnp.int32, (rows, HEAD_DIM), 1)
    cos_ref[...] = jnp.cos(ang)
    sin_ref[...] = jnp.where(lane < HEAD_DIM // 2, -1.0, 1.0) * jnp.sin(ang)


def _rope_tables(seq):
    inv_freq = ROPE_THETA ** (-jnp.arange(0, HEAD_DIM, 2, dtype=F32) / HEAD_DIM)
    freq2 = jnp.concatenate([inv_freq, inv_freq]).reshape(1, HEAD_DIM)
    rows = math.gcd(seq, 512)
    return pl.pallas_call(
        _rope_table_kernel,
        out_shape=(jax.ShapeDtypeStruct((seq, HEAD_DIM), F32),) * 2,
        grid=(seq // rows,),
        in_specs=[pl.BlockSpec((1, HEAD_DIM), lambda i: (0, 0))],
        out_specs=(pl.BlockSpec((rows, HEAD_DIM), lambda i: (i, 0)),) * 2,
        compiler_params=_cparams(("parallel",), 16 << 20),
        name="rope_tables",
    )(freq2)


def _attn_bias(nq):
    blk = ATTN_BLOCK
    kj = np.arange(3 * blk)[:, None]
    qi = np.arange(nq)[None, :] % blk
    ok = np.abs(kj - blk - qi) <= WINDOW
    variants = []
    for v in range(4):
        valid = ok & ((kj >= blk) | (v & 1 == 0)) & ((kj < 2 * blk) | (v & 2 == 0))
        variants.append(np.where(valid, 0.0, -np.inf))
    return jnp.asarray(np.stack(variants), F32)


def _attn_kernel(sink_ref, bias_ref, q_ref, kvp_ref, kvc_ref, kvn_ref, o_ref):
    n = pl.program_id(1)
    blk = ATTN_BLOCK
    kw = N_KV_HEADS * HEAD_DIM
    biases = (bias_ref[jnp.where(n == 0, 1, 0)], bias_ref[jnp.where(n == pl.num_programs(1) - 1, 2, 0)])

    for h in range(N_KV_HEADS):
        kc = slice(h * HEAD_DIM, (h + 1) * HEAD_DIM)
        vc = slice(kw + h * HEAD_DIM, kw + (h + 1) * HEAD_DIM)
        k4 = (kvp_ref[:, kc], kvc_ref[0:blk, kc], kvc_ref[blk:2 * blk, kc], kvn_ref[:, kc])
        v4 = (kvp_ref[:, vc], kvc_ref[0:blk, vc], kvc_ref[blk:2 * blk, vc], kvn_ref[:, vc])
        heads = range(h * Q_PER_KV, (h + 1) * Q_PER_KV)
        sink = jnp.concatenate([jnp.full((1, blk), sink_ref[hd] * LOG2E, F32) for hd in heads], axis=1)
        for qb in range(2):
            rows = slice(qb * blk, (qb + 1) * blk)
            k_band = jnp.concatenate(k4[qb:qb + 3], axis=0)
            v_band = jnp.concatenate(v4[qb:qb + 3], axis=0)
            q_all = jnp.concatenate([q_ref[rows, hd * HEAD_DIM:(hd + 1) * HEAD_DIM] for hd in heads], axis=0)
            s = lax.dot_general(k_band, q_all, (((1,), (1,)), ((), ())), preferred_element_type=F32) + biases[qb]
            m = jnp.maximum(jnp.max(s, axis=0, keepdims=True), sink)
            p = jnp.exp2(s - m)
            denom = jnp.sum(p, axis=0, keepdims=True) + jnp.exp2(sink - m)
            ot = lax.dot_general(v_band, p.astype(BF16), (((0,), (0,)), ((), ())), preferred_element_type=F32)
            ot = ot / denom
            for g, hd in enumerate(heads):
                o_ref[rows, hd * HEAD_DIM:(hd + 1) * HEAD_DIM] = ot[:, g * blk:(g + 1) * blk].T.astype(BF16)


def _attention(qkv, batch, seq, sink):
    blk = ATTN_BLOCK
    assert seq % (2 * blk) == 0
    nb = seq // (2 * blk)
    qw = N_Q_HEADS * HEAD_DIM
    kvw = 2 * N_KV_HEADS * HEAD_DIM
    assert qw % kvw == 0
    kv_col = qw // kvw
    nq = Q_PER_KV * blk
    grid_spec = pltpu.PrefetchScalarGridSpec(
        num_scalar_prefetch=1,
        grid=(batch, nb),
        in_specs=[
            _const_spec((4, 3 * blk, nq)),
            pl.BlockSpec((2 * blk, qw), lambda b, n, s: (b * nb + n, 0)),
            pl.BlockSpec((blk, kvw), lambda b, n, s: (2 * (b * nb) + jnp.maximum(2 * n - 1, 0), kv_col)),
            pl.BlockSpec((2 * blk, kvw), lambda b, n, s: (b * nb + n, kv_col)),
            pl.BlockSpec((blk, kvw), lambda b, n, s: (2 * (b * nb) + jnp.minimum(2 * n + 2, 2 * nb - 1), kv_col)),
        ],
        out_specs=pl.BlockSpec((2 * blk, qw), lambda b, n, s: (b * nb + n, 0)),
    )
    return pl.pallas_call(
        _attn_kernel,
        out_shape=jax.ShapeDtypeStruct((batch * seq, qw), BF16),
        grid_spec=grid_spec,
        compiler_params=_cparams(("parallel", "parallel"), 32 << 20),
        name="attention",
    )(sink, _attn_bias(nq), qkv, qkv, qkv, qkv)


def _cmul(ar, ai, br, bi):
    return ar * br - ai * bi, ar * bi + ai * br


def _cpow_table(br, bi, expo, nbits):
    pr = jnp.where((expo & 1) == 1, br, 1.0)
    pi = jnp.where((expo & 1) == 1, bi, 0.0)
    for k in range(1, nbits):
        br, bi = _cmul(br, bi, br, bi)
        bit = ((expo >> k) & 1) == 1
        nr, ni = _cmul(pr, pi, br, bi)
        pr = jnp.where(bit, nr, pr)
        pi = jnp.where(bit, ni, pi)
    return pr, pi


def _lambda_bar(lre, lim, log_dt):
    dt = jnp.exp(log_dt)
    mag = jnp.exp(lre * dt)
    return mag * jnp.cos(lim * dt), mag * jnp.sin(lim * dt)


def _ssm_weights(rowp_ref, colp_ref, btr_ref, bti_ref, crr_ref, cri_ref, ctr_ref, cti_ref, dcol_ref,
                 wz_s, wy_s, g_s):
    L, P, C = SSM_CHUNK, SSM_STATE, SSM_GROUP
    nbits = L.bit_length()

    lre, lim, ldt = rowp_ref[0, 0:1, :], rowp_ref[0, 1:2, :], rowp_ref[0, 2:3, :]
    lbr, lbi = _lambda_bar(lre, lim, ldt)
    den = lre * lre + lim * lim
    cfr = ((lbr - 1.0) * lre + lbi * lim) / den
    cfi = (lbi * lre - (lbr - 1.0) * lim) / den
    bbr, bbi = _cmul(cfr, cfi, btr_ref[0], bti_ref[0])

    dr, di = lbr, lbi
    for _ in range(L.bit_length() - 1):
        dr, di = _cmul(dr, di, dr, di)

    n_sub = lax.broadcasted_iota(jnp.int32, (L, 2 * P), 0)
    lane = lax.broadcasted_iota(jnp.int32, (L, 2 * P), 1)
    pzr, pzi = _cpow_table(lbr, lbi, jnp.where(lane < P, L - 1 - n_sub, n_sub), nbits)
    for c in range(C):
        zr, zi = _cmul(pzr, pzi, bbr[c:c + 1], bbi[c:c + 1])
        wz_s[c * L:(c + 1) * L, 0:2 * P] = zr.astype(BF16)
        wz_s[c * L:(c + 1) * L, 2 * P:4 * P] = zi.astype(BF16)

    b_rep_r = jnp.concatenate([jnp.broadcast_to(bbr[c:c + 1], (C, 2 * P)) for c in range(C)], axis=0)
    b_rep_i = jnp.concatenate([jnp.broadcast_to(bbi[c:c + 1], (C, 2 * P)) for c in range(C)], axis=0)
    c_tile_r = jnp.concatenate([crr_ref[0]] * C, axis=0)
    c_tile_i = jnp.concatenate([cri_ref[0]] * C, axis=0)
    bcr, bci = _cmul(b_rep_r, b_rep_i, c_tile_r, c_tile_i)

    clre, clim, cldt = colp_ref[0, :, 0:1], colp_ref[0, :, 1:2], colp_ref[0, :, 2:3]
    cbr, cbi = _lambda_bar(clre, clim, cldt)
    sub = lax.broadcasted_iota(jnp.int32, (2 * P, LANES), 0)
    m_lane = lax.broadcasted_iota(jnp.int32, (2 * P, LANES), 1)

    pgr, pgi = _cpow_table(cbr, cbi, jnp.abs(m_lane - (L - 1)), nbits)
    live = ((sub < P) & (m_lane >= L - 1)) | ((sub >= P) & (m_lane <= L - 1))
    pgr = jnp.where(live, pgr, 0.0)
    pgi = jnp.where(live, pgi, 0.0)
    g = _dot_exact(bcr, pgr) - _dot_exact(bci, pgi)
    g_row = lax.broadcasted_iota(jnp.int32, (C * C, LANES), 0)
    g_lane = lax.broadcasted_iota(jnp.int32, (C * C, LANES), 1)
    log2c = C.bit_length() - 1
    skip = ((g_row >> log2c) == (g_row & (C - 1))) & (g_lane == L - 1)
    g_s[...] = g + jnp.where(skip, dcol_ref[0], 0.0)

    j_lane = m_lane & (L - 1)
    pyr, pyi = _cpow_table(cbr, cbi, jnp.where(sub < P, j_lane + 1, L - j_lane), nbits)
    ctr, cti = ctr_ref[0], cti_ref[0]
    low = m_lane < L
    for a in range(C // 2):
        cr = jnp.where(low, ctr[:, 2 * a:2 * a + 1], ctr[:, 2 * a + 1:2 * a + 2])
        ci = jnp.where(low, cti[:, 2 * a:2 * a + 1], cti[:, 2 * a + 1:2 * a + 2])
        yr, yi = _cmul(cr, ci, pyr, pyi)
        wy_s[0:2 * P, a * LANES:(a + 1) * LANES] = yr.astype(BF16)
        wy_s[2 * P:4 * P, a * LANES:(a + 1) * LANES] = (-yi).astype(BF16)

    return dr, di


def _fill_toeplitz(g_s, t_s, c_ins):
    L, C = SSM_CHUNK, SSM_GROUP
    t_low = lax.broadcasted_iota(jnp.int32, (L, LANES), 1) < L
    for c_in in c_ins:
        for a in range(C // 2):
            row = c_in * C + 2 * a
            ge = jnp.broadcast_to(g_s[row:row + 1, :], (L, LANES))
            go = jnp.broadcast_to(g_s[row + 1:row + 2, :], (L, LANES))
            te = pltpu.roll(ge, L + 1, 1, stride=1, stride_axis=0)
            to = pltpu.roll(go, 1, 1, stride=1, stride_axis=0)
            t_s[c_in * L:(c_in + 1) * L, a * LANES:(a + 1) * LANES] = jnp.where(t_low, te, to).astype(BF16)


def _ssm_kernel(*refs, seqs):
    P, L, C = SSM_STATE, SSM_CHUNK, SSM_GROUP
    ns = len(seqs)
    u_refs, refs = refs[:ns], refs[ns:]
    param_refs, refs = refs[:9], refs[9:]
    y_refs, refs = refs[:ns], refs[ns:]
    t_s, wz_s, wy_s, g_s, a_s, yt_s, zr_s, zi_s, fr_s, fi_s, rr_s, ri_s = refs
    blocks = [nseq * (slen // TOK_BLOCK) for nseq, slen in seqs]
    nblk = sum(blocks)
    ar, ai = _ssm_weights(*param_refs, wz_s, wy_s, g_s)
    _fill_toeplitz(g_s, t_s, range(0, C // 2))

    blk0 = 0
    for u_ref, nb in zip(u_refs, blocks):
        a_s[blk0 * C:(blk0 + nb) * C, :] = u_ref[...].astype(F32).reshape(nb * C, TOK_BLOCK)
        blk0 += nb

    src = lax.broadcasted_iota(jnp.int32, (2 * LANES, 2 * LANES), 0)
    dst = lax.broadcasted_iota(jnp.int32, (2 * LANES, 2 * LANES), 1)
    swap = ((src & (L - 1)) | ((src & L) << 1) | ((src & LANES) >> 1)) == dst
    perm = jnp.where(swap, 1.0, 0.0).astype(BF16)
    first, second = [], []
    for a in range(C // 2):
        x01 = jnp.concatenate([a_s[pl.ds(2 * a, nblk, stride=C), :],
                               a_s[pl.ds(2 * a + 1, nblk, stride=C), :]], axis=1).astype(BF16)
        halves = _dot(x01, perm).astype(BF16)
        first.append(halves[:, 0:LANES])
        second.append(halves[:, LANES:2 * LANES])
    u = jnp.concatenate([jnp.concatenate(first, axis=1), jnp.concatenate(second, axis=1)], axis=0)

    z = _dot(u, wz_s[...])
    zr_s[...] = z[:, 0:2 * P]
    zi_s[...] = z[:, 2 * P:4 * P]

    def scan(blk0, nseq, n):
        is_fwd = lax.broadcasted_iota(jnp.int32, (nseq, 2 * P), 1) < P

        def rows(block, second_half):
            return pl.ds(blk0 + block + (nblk if second_half else 0), nseq, stride=n)

        def half(carry, f_rows, b_rows):
            sr, si = carry
            fr_s[f_rows, :] = sr
            fi_s[f_rows, :] = si
            rr_s[b_rows, :] = sr
            ri_s[b_rows, :] = si
            zr = jnp.where(is_fwd, zr_s[f_rows, :], zr_s[b_rows, :])
            zi = jnp.where(is_fwd, zi_s[f_rows, :], zi_s[b_rows, :])
            return ar * sr - ai * si + zr, ar * si + ai * sr + zi

        def step(t, carry):
            carry = half(carry, rows(t, False), rows(n - 1 - t, True))
            return half(carry, rows(t, True), rows(n - 1 - t, False))

        zero = jnp.zeros((nseq, 2 * P), F32)
        lax.fori_loop(0, n, step, (zero, zero))

    blk0 = 0
    for (nseq, slen), nb in zip(seqs, blocks):
        scan(blk0, nseq, slen // TOK_BLOCK)
        blk0 += nb

    sel = lax.broadcasted_iota(jnp.int32, fr_s.shape, 1) < P
    s_in = jnp.concatenate([jnp.where(sel, fr_s[...], rr_s[...]),
                            jnp.where(sel, fi_s[...], ri_s[...])], axis=1).astype(BF16)
    _fill_toeplitz(g_s, t_s, range(C // 2, C))
    half = CHUNK_W // 2
    y = (_dot(u[:, :half], t_s[0:half, :]) + _dot(s_in, wy_s[...])
         + _dot(u[:, half:], t_s[half:CHUNK_W, :]))

    for a in range(C // 2):
        y01 = jnp.concatenate([y[0:nblk, a * LANES:(a + 1) * LANES],
                               y[nblk:2 * nblk, a * LANES:(a + 1) * LANES]], axis=1).astype(BF16)
        chans = _dot(y01, perm)
        yt_s[pl.ds(2 * a, nblk, stride=C), :] = chans[:, 0:LANES]
        yt_s[pl.ds(2 * a + 1, nblk, stride=C), :] = chans[:, LANES:2 * LANES]
    blk0 = 0
    for y_ref, nb in zip(y_refs, blocks):
        y_ref[...] = yt_s[blk0 * C:(blk0 + nb) * C, :].reshape(nb, C, TOK_BLOCK).astype(y_ref.dtype)
        blk0 += nb


def _ssm(uts, seqs, lam_re, lam_im, log_dt, b_re, b_im, c_re, c_im, d_skip):
    width = uts[0].shape[1]
    _, groups, P = lam_re.shape
    C = SSM_GROUP
    assert P == SSM_STATE and b_re.shape[-1] == C and groups * C == width
    assert all(s % TOK_BLOCK == 0 for _, s in seqs)
    assert all(ut.shape == (b * s // TOK_BLOCK, width, TOK_BLOCK) for ut, (b, s) in zip(uts, seqs))
    nblk = sum(ut.shape[0] for ut in uts)

    def lanes_fb(a):
        return jnp.transpose(a, (1, 0, 2)).reshape(groups, 2 * P)

    ldt = jnp.broadcast_to(log_dt[:, :, None], (2, groups, P))
    rowp = jnp.stack([lanes_fb(lam_re), lanes_fb(lam_im), lanes_fb(ldt)], axis=1)
    colp = jnp.transpose(rowp, (0, 2, 1))
    bt = lambda b: jnp.transpose(b, (1, 3, 0, 2)).reshape(groups, C, 2 * P)
    cr = lambda c: jnp.transpose(c, (1, 2, 0, 3)).reshape(groups, C, 2 * P)
    ct = lambda c: jnp.transpose(c, (1, 0, 3, 2)).reshape(groups, 2 * P, C)
    dcol = jnp.tile(d_skip.reshape(groups, 1, C), (1, C, 1)).reshape(groups, C * C, 1)

    g3 = lambda s1, s2: pl.BlockSpec((1, s1, s2), lambda g: (g, 0, 0))
    acts = [pl.BlockSpec((ut.shape[0], C, TOK_BLOCK), lambda g: (0, g, 0)) for ut in uts]
    rows2 = 2 * nblk
    return pl.pallas_call(
        functools.partial(_ssm_kernel, seqs=tuple(seqs)),
        out_shape=[jax.ShapeDtypeStruct(ut.shape, BF16) for ut in uts],
        grid=(groups,),
        in_specs=acts + [g3(3, 2 * P), g3(2 * P, 3), g3(C, 2 * P), g3(C, 2 * P), g3(C, 2 * P), g3(C, 2 * P),
                         g3(2 * P, C), g3(2 * P, C), g3(C * C, 1)],
        out_specs=acts,
        scratch_shapes=[
            pltpu.VMEM((CHUNK_W, CHUNK_W), BF16),
            pltpu.VMEM((CHUNK_W, 4 * P), BF16),
            pltpu.VMEM((4 * P, CHUNK_W), BF16),
            pltpu.VMEM((C * C, LANES), F32),
            pltpu.VMEM((nblk * C, TOK_BLOCK), F32),
            pltpu.VMEM((nblk * C, TOK_BLOCK), F32),
        ] + [pltpu.VMEM((rows2, 2 * P), F32)] * 6,
        compiler_params=_cparams(("parallel",), 40 << 20),
        name="ssm",
    )(*uts, rowp, colp, bt(b_re), bt(b_im), cr(c_re), cr(c_im), ct(c_re), ct(c_im), dcol)


def _merge_kernel(x_ref, a_ref, yt_ref, gate_ref, wglu_ref, bglu_ref, wao_ref, wso_ref, wout_ref, o_ref):
    d = x_ref.shape[1]
    y = jnp.concatenate([yt_ref[j].astype(F32).T for j in range(yt_ref.shape[0])], axis=0)
    z = jax.nn.gelu(y)
    z = z * jax.nn.sigmoid(_dot(z.astype(BF16), wglu_ref[...]) + bglu_ref[...])
    m = _dot(z.astype(BF16), wso_ref[...])
    a = _dot(a_ref[...], wao_ref[...])
    g_attn, g_ssm = gate_ref[:, 0:d].astype(F32), gate_ref[:, d:2 * d].astype(F32)
    merged = jax.nn.sigmoid(g_attn) * a + jax.nn.sigmoid(g_ssm) * m
    o_ref[...] = x_ref[...] + _dot(merged.astype(BF16), wout_ref[...])


def _merge(x, attn, yt, gates, w_glu, b_glu, w_ao, w_so, w_out):
    t, d = x.shape
    aw, sw = attn.shape[1], yt.shape[1]
    tm = MERGE_ROWS
    weights = (sw * sw + aw * d + sw * d + d * d) * 2
    vmem = weights + 2 * tm * (2 * d * 4 + (aw + sw + 2 * d) * 2) + 6 * tm * d * 4 + (4 << 20)
    return pl.pallas_call(
        _merge_kernel,
        out_shape=jax.ShapeDtypeStruct((t, d), F32),
        grid=(t // tm,),
        in_specs=[
            pl.BlockSpec((tm, d), lambda i: (i, 0)),
            pl.BlockSpec((tm, aw), lambda i: (i, 0)),
            pl.BlockSpec((tm // TOK_BLOCK, sw, TOK_BLOCK), lambda i: (i, 0, 0)),
            pl.BlockSpec((tm, 2 * d), lambda i: (i, 0)),
            _const_spec((sw, sw)), _const_spec((1, sw)), _const_spec((aw, d)), _const_spec((sw, d)),
            _const_spec((d, d)),
        ],
        out_specs=pl.BlockSpec((tm, d), lambda i: (i, 0)),
        compiler_params=_cparams(("parallel",), vmem),
        name="mixer_out",
    )(x, attn, yt, gates, w_glu, b_glu.reshape(1, sw), w_ao, w_so, w_out)


def _layer(xs, ffn1_norm, ffn1_w_gate_up, ffn1_w_down, mix_norm, w_in, q_norm, k_norm, attn_sink,
           ssm_lambda_re, ssm_lambda_im, ssm_log_dt, ssm_b_re, ssm_b_im, ssm_c_re, ssm_c_im, ssm_d,
           w_glu, b_glu, w_attn_out, w_ssm_out, w_out, ffn2_norm, ffn2_w_gate_up, ffn2_w_down):
    d = xs[0].shape[-1]
    seqs = [(x.shape[0], x.shape[1]) for x in xs]
    aw = N_Q_HEADS * HEAD_DIM
    kw = N_KV_HEADS * HEAD_DIM
    sw = ssm_d.shape[0]

    o_u, o_g = aw + 2 * kw, aw + 2 * kw + sw
    w_qkv = w_in[:, :o_u].astype(BF16)
    w_u_t = w_in[:, o_u:o_g].T.astype(BF16)
    w_gates = w_in[:, o_g:].astype(BF16)
    w1_gu, w1_d = _ffn_weights(ffn1_w_gate_up, ffn1_w_down)
    w2_gu, w2_d = _ffn_weights(ffn2_w_gate_up, ffn2_w_down)
    w_glu_b, w_ao, w_so, w_o = (w.astype(BF16) for w in (w_glu, w_attn_out, w_ssm_out, w_out))
    cos2, sin2 = _rope_tables(max(s for _, s in seqs))

    x1s, attns, gates, uts = [], [], [], []
    for x, (b, s) in zip(xs, seqs):
        x1 = _ffn(x.reshape(b * s, d), ffn1_norm, w1_gu, w1_d)
        qkv, ut, gate = _inproj(x1, mix_norm, w_qkv, w_u_t, w_gates, s, cos2, sin2, q_norm, k_norm)
        uts.append(ut)
        gates.append(gate)
        attns.append(_attention(qkv, b, s, attn_sink))
        x1s.append(x1)

    yts = _ssm(uts, seqs, ssm_lambda_re, ssm_lambda_im, ssm_log_dt, ssm_b_re, ssm_b_im, ssm_c_re, ssm_c_im, ssm_d)

    outs = []
    for x_in, x1, attn, yt, gate in zip(xs, x1s, attns, yts, gates):
        x2 = _merge(x1, attn, yt, gate, w_glu_b, b_glu, w_ao, w_so, w_o)
        outs.append(_ffn(x2, ffn2_norm, w2_gu, w2_d).reshape(x_in.shape))
    return tuple(outs)


def kernel(x_prompt, x_sample, ffn1_norm, ffn1_w_gate_up, ffn1_w_down, mix_norm, w_in, q_norm, k_norm, attn_sink, ssm_lambda_re, ssm_lambda_im, ssm_log_dt, ssm_b_re, ssm_b_im, ssm_c_re, ssm_c_im, ssm_d, w_glu, b_glu, w_attn_out, w_ssm_out, w_out, ffn2_norm, ffn2_w_gate_up, ffn2_w_down):
    xs = (x_prompt, x_sample)
    depth = ffn1_norm.shape[0]
    params = (ffn1_norm, ffn1_w_gate_up, ffn1_w_down, mix_norm, w_in, q_norm, k_norm, attn_sink,
              ssm_lambda_re, ssm_lambda_im, ssm_log_dt, ssm_b_re, ssm_b_im, ssm_c_re, ssm_c_im, ssm_d,
              w_glu, b_glu, w_attn_out, w_ssm_out, w_out, ffn2_norm, ffn2_w_gate_up, ffn2_w_down)
    for l in range(depth):
        xs = _layer(xs, *(p[l] for p in params))
    return xs
```

```python
import functools
import math

import jax
import jax.numpy as jnp
import numpy as np
from jax import lax
from jax.experimental import pallas as pl
from jax.experimental.pallas import tpu as pltpu

F32 = jnp.float32
BF16 = jnp.bfloat16

HEAD_DIM = 128
N_Q_HEADS = 8
N_KV_HEADS = 2
Q_PER_KV = N_Q_HEADS // N_KV_HEADS
WINDOW = 128
ATTN_BLOCK = 128
ROPE_THETA = 10000.0
SSM_GROUP = 16
SSM_STATE = 64
EPS = 1e-6
LOG2E = math.log2(math.e)

LANES = 128
VMEM_LIMIT_CAP = 56 * 1024 * 1024

SSM_CHUNK = 64
CHUNK_W = SSM_CHUNK * SSM_GROUP
TOK_BLOCK = 128
FFN_ROWS = 512
FFN_COLS = 512
INPROJ_ROWS = 256
MERGE_ROWS = 256
ATTN_QBLOCKS = 4

assert 2 * SSM_CHUNK == TOK_BLOCK == LANES and 2 * SSM_STATE == LANES


def _cparams(semantics, vmem_bytes):
    return pltpu.CompilerParams(dimension_semantics=semantics,
                                vmem_limit_bytes=min(int(vmem_bytes), VMEM_LIMIT_CAP))


def _rms(x, gain):
    return x * lax.rsqrt(jnp.mean(x * x, axis=-1, keepdims=True) + EPS) * gain


def _dot(a, b):
    return jnp.dot(a, b, preferred_element_type=F32)


def _dot_exact(a, b):
    return jnp.dot(a, b, preferred_element_type=F32, precision=lax.Precision.HIGHEST)


def _ffn_kernel(x_ref, gain_ref, wg_ref, wu_ref, wd_ref, o_ref, h_ref):
    @pl.when(pl.program_id(1) == 0)
    def _():
        x = x_ref[...]
        h_ref[...] = _rms(x, gain_ref[...]).astype(BF16)
        o_ref[...] = x

    h = h_ref[...]
    g = _dot(h, wg_ref[...])
    u = _dot(h, wu_ref[...])
    a = (g * jax.nn.sigmoid(g)) * u
    o_ref[...] += _dot(a.astype(BF16), wd_ref[...])


def _ffn_weights(w_gate_up, w_down):
    return w_gate_up.astype(BF16), (0.5 * w_down).astype(BF16)


def _ffn(x, gain, w_gate_up, w_down_half):
    t, d = x.shape
    d_ff = w_down_half.shape[0]
    tm, tf = FFN_ROWS, FFN_COLS
    nf = d_ff // tf
    vmem = (4 * tm * d * 4) + (tm * d * 2) + 2 * 3 * (d * tf * 2) + 4 * (tm * tf * 4) + (4 << 20)
    return pl.pallas_call(
        _ffn_kernel,
        out_shape=jax.ShapeDtypeStruct((t, d), F32),
        grid=(t // tm, nf),
        in_specs=[
            pl.BlockSpec((tm, d), lambda i, f: (i, 0)),
            pl.BlockSpec((1, d), lambda i, f: (0, 0)),
            pl.BlockSpec((d, tf), lambda i, f: (0, f)),
            pl.BlockSpec((d, tf), lambda i, f: (0, f + nf)),
            pl.BlockSpec((tf, d), lambda i, f: (f, 0)),
        ],
        out_specs=pl.BlockSpec((tm, d), lambda i, f: (i, 0)),
        scratch_shapes=[pltpu.VMEM((tm, d), BF16)],
        compiler_params=_cparams(("parallel", "arbitrary"), vmem),
        name="ffn",
    )(x, gain.reshape(1, d), w_gate_up, w_gate_up, w_down_half)


def _rope(x, cos2, sin2):
    return x * cos2 + pltpu.roll(x, HEAD_DIM // 2, 1) * sin2


def _inproj_kernel(x_ref, gain_ref, wqkv_ref, wut_ref, wg_ref, cos_ref, sin_ref, qg_ref, kg_ref,
                   qkv_ref, ut_ref, gates_ref):
    h = _rms(x_ref[...], gain_ref[...]).astype(BF16)

    r = _dot(h, wqkv_ref[...])
    cos2, sin2 = cos_ref[...], sin_ref[...]
    qg = qg_ref[...] * (LOG2E * HEAD_DIM ** -0.5)
    kg = kg_ref[...]
    n_rot = N_Q_HEADS + N_KV_HEADS
    for head in range(n_rot):
        cols = slice(head * HEAD_DIM, (head + 1) * HEAD_DIM)
        gain = qg if head < N_Q_HEADS else kg
        qkv_ref[:, cols] = _rope(_rms(r[:, cols], gain), cos2, sin2).astype(BF16)
    vcols = slice(n_rot * HEAD_DIM, (n_rot + N_KV_HEADS) * HEAD_DIM)
    qkv_ref[:, vcols] = r[:, vcols].astype(BF16)

    rt = lax.dot_general(wut_ref[...], h, (((1,), (1,)), ((), ())), preferred_element_type=F32)
    for j in range(ut_ref.shape[0]):
        ut_ref[j] = rt[:, j * TOK_BLOCK:(j + 1) * TOK_BLOCK].astype(BF16)

    gates_ref[...] = _dot(h, wg_ref[...]).astype(BF16)


def _const_spec(shape):
    return pl.BlockSpec(shape, lambda *_: (0,) * len(shape), pipeline_mode=pl.Buffered(1))


def _inproj(x, gain, w_qkv, w_u_t, w_gates, seq, cos2, sin2, q_gain, k_gain):
    t, d = x.shape
    nqkv, nu, ng = w_qkv.shape[1], w_u_t.shape[0], w_gates.shape[1]
    tm = math.gcd(INPROJ_ROWS, seq)
    tiles_per_seq = seq // tm
    n_all = nqkv + nu + ng
    vmem = d * n_all * 2 + 2 * tm * d * 4 + 2 * tm * n_all * 2 + tm * n_all * 4 + tm * d * 6 + (6 << 20)
    row = lambda n: pl.BlockSpec((tm, n), lambda i: (i, 0))
    table = pl.BlockSpec((tm, HEAD_DIM), lambda i: (i % tiles_per_seq, 0))
    return pl.pallas_call(
        _inproj_kernel,
        out_shape=(jax.ShapeDtypeStruct((t, nqkv), BF16),
                   jax.ShapeDtypeStruct((t // TOK_BLOCK, nu, TOK_BLOCK), BF16),
                   jax.ShapeDtypeStruct((t, ng), BF16)),
        grid=(t // tm,),
        in_specs=[row(d), _const_spec((1, d)), _const_spec((d, nqkv)), _const_spec((nu, d)), _const_spec((d, ng)),
                  table, table, _const_spec((1, HEAD_DIM)), _const_spec((1, HEAD_DIM))],
        out_specs=(row(nqkv), pl.BlockSpec((tm // TOK_BLOCK, nu, TOK_BLOCK), lambda i: (i, 0, 0)), row(ng)),
        compiler_params=_cparams(("parallel",), vmem),
        name="inproj",
    )(x, gain.reshape(1, d), w_qkv, w_u_t, w_gates, cos2, sin2,
      q_gain.reshape(1, HEAD_DIM), k_gain.reshape(1, HEAD_DIM))


def _rope_table_kernel(freq_ref, cos_ref, sin_ref):
    rows = cos_ref.shape[0]
    pos = (lax.broadcasted_iota(jnp.int32, (rows, HEAD_DIM), 0) + pl.program_id(0) * rows).astype(F32)
    ang = pos * freq_ref[...]
    lane = lax.broadcasted_iota(jnp.int32, (rows, HEAD_DIM), 1)
    cos_ref[...] = jnp.cos(ang)
    sin_ref[...] = jnp.where(lane < HEAD_DIM // 2, -1.0, 1.0) * jnp.sin(ang)


def _rope_tables(seq):
    inv_freq = ROPE_THETA ** (-jnp.arange(0, HEAD_DIM, 2, dtype=F32) / HEAD_DIM)
    freq2 = jnp.concatenate([inv_freq, inv_freq]).reshape(1, HEAD_DIM)
    rows = math.gcd(seq, 512)
    return pl.pallas_call(
        _rope_table_kernel,
        out_shape=(jax.ShapeDtypeStruct((seq, HEAD_DIM), F32),) * 2,
        grid=(seq // rows,),
        in_specs=[pl.BlockSpec((1, HEAD_DIM), lambda i: (0, 0))],
        out_specs=(pl.BlockSpec((rows, HEAD_DIM), lambda i: (i, 0)),) * 2,
        compiler_params=_cparams(("parallel",), 16 << 20),
        name="rope_tables",
    )(freq2)


def _attn_bias(nq):
    blk = ATTN_BLOCK
    kj = np.arange(3 * blk)[:, None]
    qi = np.arange(nq)[None, :] % blk
    ok = np.abs(kj - blk - qi) <= WINDOW
    variants = []
    for v in range(4):
        valid = ok & ((kj >= blk) | (v & 1 == 0)) & ((kj < 2 * blk) | (v & 2 == 0))
        variants.append(np.where(valid, 0.0, -np.inf))
    return jnp.asarray(np.stack(variants), F32)


def _attn_kernel(sink_ref, bias_ref, q_ref, kvp_ref, kvc_ref, kvn_ref, o_ref):
    n = pl.program_id(1)
    blk = ATTN_BLOCK
    nqb = q_ref.shape[0] // blk
    kw = N_KV_HEADS * HEAD_DIM
    no_prev = jnp.where(n == 0, 1, 0)
    no_next = jnp.where(n == pl.num_programs(1) - 1, 2, 0)

    for h in range(N_KV_HEADS):
        kc = slice(h * HEAD_DIM, (h + 1) * HEAD_DIM)
        vc = slice(kw + h * HEAD_DIM, kw + (h + 1) * HEAD_DIM)
        blocks = [kvp_ref] + [kvc_ref.at[qb * blk:(qb + 1) * blk] for qb in range(nqb)] + [kvn_ref]
        k_parts = [ref[:, kc] for ref in blocks]
        v_parts = [ref[:, vc] for ref in blocks]
        heads = range(h * Q_PER_KV, (h + 1) * Q_PER_KV)
        sink = jnp.concatenate([jnp.full((1, blk), sink_ref[hd] * LOG2E, F32) for hd in heads], axis=1)
        for qb in range(nqb):
            rows = slice(qb * blk, (qb + 1) * blk)
            variant = (no_prev if qb == 0 else 0) + (no_next if qb == nqb - 1 else 0)
            k_band = jnp.concatenate(k_parts[qb:qb + 3], axis=0)
            v_band = jnp.concatenate(v_parts[qb:qb + 3], axis=0)
            q_all = jnp.concatenate([q_ref[rows, hd * HEAD_DIM:(hd + 1) * HEAD_DIM] for hd in heads], axis=0)
            s = lax.dot_general(k_band, q_all, (((1,), (1,)), ((), ())), preferred_element_type=F32)
            s = s + bias_ref[variant]
            m = jnp.maximum(jnp.max(s, axis=0, keepdims=True), sink)
            p = jnp.exp2(s - m)
            denom = jnp.sum(p, axis=0, keepdims=True) + jnp.exp2(sink - m)
            ot = lax.dot_general(v_band, p.astype(BF16), (((0,), (0,)), ((), ())), preferred_element_type=F32)
            ot = ot / denom
            for g, hd in enumerate(heads):
                o_ref[rows, hd * HEAD_DIM:(hd + 1) * HEAD_DIM] = ot[:, g * blk:(g + 1) * blk].T.astype(BF16)


def _attention(qkv, batch, seq, sink):
    blk = ATTN_BLOCK
    nqb = math.gcd(ATTN_QBLOCKS, seq // blk)
    step = nqb * blk
    nb = seq // step
    qw = N_Q_HEADS * HEAD_DIM
    kvw = 2 * N_KV_HEADS * HEAD_DIM
    assert qw % kvw == 0 and seq % blk == 0
    kv_col = qw // kvw
    nq = Q_PER_KV * blk
    grid_spec = pltpu.PrefetchScalarGridSpec(
        num_scalar_prefetch=1,
        grid=(batch, nb),
        in_specs=[
            _const_spec((4, 3 * blk, nq)),
            pl.BlockSpec((step, qw), lambda b, n, s: (b * nb + n, 0)),
            pl.BlockSpec((blk, kvw), lambda b, n, s: (nqb * (b * nb) + jnp.maximum(nqb * n - 1, 0), kv_col)),
            pl.BlockSpec((step, kvw), lambda b, n, s: (b * nb + n, kv_col)),
            pl.BlockSpec((blk, kvw),
                         lambda b, n, s: (nqb * (b * nb) + jnp.minimum(nqb * n + nqb, nqb * nb - 1), kv_col)),
        ],
        out_specs=pl.BlockSpec((step, qw), lambda b, n, s: (b * nb + n, 0)),
    )
    return pl.pallas_call(
        _attn_kernel,
        out_shape=jax.ShapeDtypeStruct((batch * seq, qw), BF16),
        grid_spec=grid_spec,
        compiler_params=_cparams(("parallel", "parallel"), 40 << 20),
        name="attention",
    )(sink, _attn_bias(nq), qkv, qkv, qkv, qkv)


def _cmul(ar, ai, br, bi):
    return ar * br - ai * bi, ar * bi + ai * br


def _cpow_table(br, bi, expo, nbits):
    pr = jnp.where((expo & 1) == 1, br, 1.0)
    pi = jnp.where((expo & 1) == 1, bi, 0.0)
    for k in range(1, nbits):
        br, bi = _cmul(br, bi, br, bi)
        bit = ((expo >> k) & 1) == 1
        nr, ni = _cmul(pr, pi, br, bi)
        pr = jnp.where(bit, nr, pr)
        pi = jnp.where(bit, ni, pi)
    return pr, pi


def _lambda_bar(lre, lim, log_dt):
    dt = jnp.exp(log_dt)
    mag = jnp.exp(lre * dt)
    return mag * jnp.cos(lim * dt), mag * jnp.sin(lim * dt)


def _ssm_weights(rowp_ref, colp_ref, btr_ref, bti_ref, crr_ref, cri_ref, ctr_ref, cti_ref, dcol_ref,
                 wz_s, wy_s, g_s):
    L, P, C = SSM_CHUNK, SSM_STATE, SSM_GROUP
    nbits = L.bit_length()

    lre, lim, ldt = rowp_ref[0, 0:1, :], rowp_ref[0, 1:2, :], rowp_ref[0, 2:3, :]
    lbr, lbi = _lambda_bar(lre, lim, ldt)
    den = lre * lre + lim * lim
    cfr = ((lbr - 1.0) * lre + lbi * lim) / den
    cfi = (lbi * lre - (lbr - 1.0) * lim) / den
    bbr, bbi = _cmul(cfr, cfi, btr_ref[0], bti_ref[0])

    dr, di = lbr, lbi
    for _ in range(L.bit_length() - 1):
        dr, di = _cmul(dr, di, dr, di)

    n_sub = lax.broadcasted_iota(jnp.int32, (L, 2 * P), 0)
    lane = lax.broadcasted_iota(jnp.int32, (L, 2 * P), 1)
    pzr, pzi = _cpow_table(lbr, lbi, jnp.where(lane < P, L - 1 - n_sub, n_sub), nbits)
    for c in range(C):
        zr, zi = _cmul(pzr, pzi, bbr[c:c + 1], bbi[c:c + 1])
        wz_s[c * L:(c + 1) * L, 0:2 * P] = zr.astype(BF16)
        wz_s[c * L:(c + 1) * L, 2 * P:4 * P] = zi.astype(BF16)

    b_rep_r = jnp.concatenate([jnp.broadcast_to(bbr[c:c + 1], (C, 2 * P)) for c in range(C)], axis=0)
    b_rep_i = jnp.concatenate([jnp.broadcast_to(bbi[c:c + 1], (C, 2 * P)) for c in range(C)], axis=0)
    c_tile_r = jnp.concatenate([crr_ref[0]] * C, axis=0)
    c_tile_i = jnp.concatenate([cri_ref[0]] * C, axis=0)
    bcr, bci = _cmul(b_rep_r, b_rep_i, c_tile_r, c_tile_i)

    clre, clim, cldt = colp_ref[0, :, 0:1], colp_ref[0, :, 1:2], colp_ref[0, :, 2:3]
    cbr, cbi = _lambda_bar(clre, clim, cldt)
    sub = lax.broadcasted_iota(jnp.int32, (2 * P, LANES), 0)
    m_lane = lax.broadcasted_iota(jnp.int32, (2 * P, LANES), 1)

    pgr, pgi = _cpow_table(cbr, cbi, jnp.abs(m_lane - (L - 1)), nbits)
    live = ((sub < P) & (m_lane >= L - 1)) | ((sub >= P) & (m_lane <= L - 1))
    pgr = jnp.where(live, pgr, 0.0)
    pgi = jnp.where(live, pgi, 0.0)
    g = _dot_exact(bcr, pgr) - _dot_exact(bci, pgi)
    g_row = lax.broadcasted_iota(jnp.int32, (C * C, LANES), 0)
    g_lane = lax.broadcasted_iota(jnp.int32, (C * C, LANES), 1)
    log2c = C.bit_length() - 1
    skip = ((g_row >> log2c) == (g_row & (C - 1))) & (g_lane == L - 1)
    g_s[...] = g + jnp.where(skip, dcol_ref[0], 0.0)

    j_lane = m_lane & (L - 1)
    pyr, pyi = _cpow_table(cbr, cbi, jnp.where(sub < P, j_lane + 1, L - j_lane), nbits)
    ctr, cti = ctr_ref[0], cti_ref[0]
    low = m_lane < L
    for a in range(C // 2):
        cr = jnp.where(low, ctr[:, 2 * a:2 * a + 1], ctr[:, 2 * a + 1:2 * a + 2])
        ci = jnp.where(low, cti[:, 2 * a:2 * a + 1], cti[:, 2 * a + 1:2 * a + 2])
        yr, yi = _cmul(cr, ci, pyr, pyi)
        wy_s[0:2 * P, a * LANES:(a + 1) * LANES] = yr.astype(BF16)
        wy_s[2 * P:4 * P, a * LANES:(a + 1) * LANES] = (-yi).astype(BF16)

    return dr, di


def _fill_toeplitz(g_s, t_s, c_ins):
    L, C = SSM_CHUNK, SSM_GROUP
    t_low = lax.broadcasted_iota(jnp.int32, (L, LANES), 1) < L
    for c_in in c_ins:
        for a in range(C // 2):
            row = c_in * C + 2 * a
            ge = jnp.broadcast_to(g_s[row:row + 1, :], (L, LANES))
            go = jnp.broadcast_to(g_s[row + 1:row + 2, :], (L, LANES))
            te = pltpu.roll(ge, L + 1, 1, stride=1, stride_axis=0)
            to = pltpu.roll(go, 1, 1, stride=1, stride_axis=0)
            t_s[c_in * L:(c_in + 1) * L, a * LANES:(a + 1) * LANES] = jnp.where(t_low, te, to).astype(BF16)


def _ssm_kernel(*refs, seqs):
    P, L, C = SSM_STATE, SSM_CHUNK, SSM_GROUP
    ns = len(seqs)
    u_refs, refs = refs[:ns], refs[ns:]
    param_refs, refs = refs[:9], refs[9:]
    y_refs, refs = refs[:ns], refs[ns:]
    t_s, wz_s, wy_s, g_s, a_s, yt_s, zr_s, zi_s, fr_s, fi_s, rr_s, ri_s = refs
    blocks = [nseq * (slen // TOK_BLOCK) for nseq, slen in seqs]
    nblk = sum(blocks)
    ar, ai = _ssm_weights(*param_refs, wz_s, wy_s, g_s)
    _fill_toeplitz(g_s, t_s, range(0, C // 2))

    blk0 = 0
    for u_ref, nb in zip(u_refs, blocks):
        a_s[blk0 * C:(blk0 + nb) * C, :] = u_ref[...].astype(F32).reshape(nb * C, TOK_BLOCK)
        blk0 += nb

    src = lax.broadcasted_iota(jnp.int32, (2 * LANES, 2 * LANES), 0)
    dst = lax.broadcasted_iota(jnp.int32, (2 * LANES, 2 * LANES), 1)
    swap = ((src & (L - 1)) | ((src & L) << 1) | ((src & LANES) >> 1)) == dst
    perm = jnp.where(swap, 1.0, 0.0).astype(BF16)
    first, second = [], []
    for a in range(C // 2):
        x01 = jnp.concatenate([a_s[pl.ds(2 * a, nblk, stride=C), :],
                               a_s[pl.ds(2 * a + 1, nblk, stride=C), :]], axis=1).astype(BF16)
        halves = _dot(x01, perm).astype(BF16)
        first.append(halves[:, 0:LANES])
        second.append(halves[:, LANES:2 * LANES])
    u = jnp.concatenate([jnp.concatenate(first, axis=1), jnp.concatenate(second, axis=1)], axis=0)

    z = _dot(u, wz_s[...])
    zr_s[...] = z[:, 0:2 * P]
    zi_s[...] = z[:, 2 * P:4 * P]

    def scan(blk0, nseq, n):
        is_fwd = lax.broadcasted_iota(jnp.int32, (nseq, 2 * P), 1) < P

        def rows(block, second_half):
            return pl.ds(blk0 + block + (nblk if second_half else 0), nseq, stride=n)

        def half(carry, f_rows, b_rows):
            sr, si = carry
            fr_s[f_rows, :] = sr
            fi_s[f_rows, :] = si
            rr_s[b_rows, :] = sr
            ri_s[b_rows, :] = si
            zr = jnp.where(is_fwd, zr_s[f_rows, :], zr_s[b_rows, :])
            zi = jnp.where(is_fwd, zi_s[f_rows, :], zi_s[b_rows, :])
            return ar * sr - ai * si + zr, ar * si + ai * sr + zi

        def step(t, carry):
            carry = half(carry, rows(t, False), rows(n - 1 - t, True))
            return half(carry, rows(t, True), rows(n - 1 - t, False))

        zero = jnp.zeros((nseq, 2 * P), F32)
        lax.fori_loop(0, n, step, (zero, zero))

    blk0 = 0
    for (nseq, slen), nb in zip(seqs, blocks):
        scan(blk0, nseq, slen // TOK_BLOCK)
        blk0 += nb

    sel = lax.broadcasted_iota(jnp.int32, fr_s.shape, 1) < P
    s_in = jnp.concatenate([jnp.where(sel, fr_s[...], rr_s[...]),
                            jnp.where(sel, fi_s[...], ri_s[...])], axis=1).astype(BF16)
    _fill_toeplitz(g_s, t_s, range(C // 2, C))
    half = CHUNK_W // 2
    y = (_dot(u[:, :half], t_s[0:half, :]) + _dot(s_in, wy_s[...])
         + _dot(u[:, half:], t_s[half:CHUNK_W, :]))

    for a in range(C // 2):
        y01 = jnp.concatenate([y[0:nblk, a * LANES:(a + 1) * LANES],
                               y[nblk:2 * nblk, a * LANES:(a + 1) * LANES]], axis=1).astype(BF16)
        chans = _dot(y01, perm)
        yt_s[pl.ds(2 * a, nblk, stride=C), :] = chans[:, 0:LANES]
        yt_s[pl.ds(2 * a + 1, nblk, stride=C), :] = chans[:, LANES:2 * LANES]
    blk0 = 0
    for y_ref, nb in zip(y_refs, blocks):
        y_ref[...] = yt_s[blk0 * C:(blk0 + nb) * C, :].reshape(nb, C, TOK_BLOCK).astype(y_ref.dtype)
        blk0 += nb


def _ssm(uts, seqs, lam_re, lam_im, log_dt, b_re, b_im, c_re, c_im, d_skip):
    width = uts[0].shape[1]
    _, groups, P = lam_re.shape
    C = SSM_GROUP
    assert P == SSM_STATE and b_re.shape[-1] == C and groups * C == width
    assert all(s % TOK_BLOCK == 0 for _, s in seqs)
    assert all(ut.shape == (b * s // TOK_BLOCK, width, TOK_BLOCK) for ut, (b, s) in zip(uts, seqs))
    nblk = sum(ut.shape[0] for ut in uts)

    def lanes_fb(a):
        return jnp.transpose(a, (1, 0, 2)).reshape(groups, 2 * P)

    ldt = jnp.broadcast_to(log_dt[:, :, None], (2, groups, P))
    rowp = jnp.stack([lanes_fb(lam_re), lanes_fb(lam_im), lanes_fb(ldt)], axis=1)
    colp = jnp.transpose(rowp, (0, 2, 1))
    bt = lambda b: jnp.transpose(b, (1, 3, 0, 2)).reshape(groups, C, 2 * P)
    cr = lambda c: jnp.transpose(c, (1, 2, 0, 3)).reshape(groups, C, 2 * P)
    ct = lambda c: jnp.transpose(c, (1, 0, 3, 2)).reshape(groups, 2 * P, C)
    dcol = jnp.tile(d_skip.reshape(groups, 1, C), (1, C, 1)).reshape(groups, C * C, 1)

    g3 = lambda s1, s2: pl.BlockSpec((1, s1, s2), lambda g: (g, 0, 0))
    acts = [pl.BlockSpec((ut.shape[0], C, TOK_BLOCK), lambda g: (0, g, 0)) for ut in uts]
    rows2 = 2 * nblk
    return pl.pallas_call(
        functools.partial(_ssm_kernel, seqs=tuple(seqs)),
        out_shape=[jax.ShapeDtypeStruct(ut.shape, BF16) for ut in uts],
        grid=(groups,),
        in_specs=acts + [g3(3, 2 * P), g3(2 * P, 3), g3(C, 2 * P), g3(C, 2 * P), g3(C, 2 * P), g3(C, 2 * P),
                         g3(2 * P, C), g3(2 * P, C), g3(C * C, 1)],
        out_specs=acts,
        scratch_shapes=[
            pltpu.VMEM((CHUNK_W, CHUNK_W), BF16),
            pltpu.VMEM((CHUNK_W, 4 * P), BF16),
            pltpu.VMEM((4 * P, CHUNK_W), BF16),
            pltpu.VMEM((C * C, LANES), F32),
            pltpu.VMEM((nblk * C, TOK_BLOCK), F32),
            pltpu.VMEM((nblk * C, TOK_BLOCK), F32),
        ] + [pltpu.VMEM((rows2, 2 * P), F32)] * 6,
        compiler_params=_cparams(("parallel",), 40 << 20),
        name="ssm",
    )(*uts, rowp, colp, bt(b_re), bt(b_im), cr(c_re), cr(c_im), ct(c_re), ct(c_im), dcol)


def _merge_kernel(x_ref, a_ref, yt_ref, gate_ref, wglu_ref, bglu_ref, wao_ref, wso_ref, wout_ref, o_ref):
    d = x_ref.shape[1]
    y = jnp.concatenate([yt_ref[j].astype(F32).T for j in range(yt_ref.shape[0])], axis=0)
    z = jax.nn.gelu(y)
    z = z * jax.nn.sigmoid(_dot(z.astype(BF16), wglu_ref[...]) + bglu_ref[...])
    m = _dot(z.astype(BF16), wso_ref[...])
    a = _dot(a_ref[...], wao_ref[...])
    g_attn, g_ssm = gate_ref[:, 0:d].astype(F32), gate_ref[:, d:2 * d].astype(F32)
    merged = jax.nn.sigmoid(g_attn) * a + jax.nn.sigmoid(g_ssm) * m
    o_ref[...] = x_ref[...] + _dot(merged.astype(BF16), wout_ref[...])


def _merge(x, attn, yt, gates, w_glu, b_glu, w_ao, w_so, w_out):
    t, d = x.shape
    aw, sw = attn.shape[1], yt.shape[1]
    tm = MERGE_ROWS
    weights = (sw * sw + aw * d + sw * d + d * d) * 2
    vmem = weights + 2 * tm * (2 * d * 4 + (aw + sw + 2 * d) * 2) + 6 * tm * d * 4 + (4 << 20)
    return pl.pallas_call(
        _merge_kernel,
        out_shape=jax.ShapeDtypeStruct((t, d), F32),
        grid=(t // tm,),
        in_specs=[
            pl.BlockSpec((tm, d), lambda i: (i, 0)),
            pl.BlockSpec((tm, aw), lambda i: (i, 0)),
            pl.BlockSpec((tm // TOK_BLOCK, sw, TOK_BLOCK), lambda i: (i, 0, 0)),
            pl.BlockSpec((tm, 2 * d), lambda i: (i, 0)),
            _const_spec((sw, sw)), _const_spec((1, sw)), _const_spec((aw, d)), _const_spec((sw, d)),
            _const_spec((d, d)),
        ],
        out_specs=pl.BlockSpec((tm, d), lambda i: (i, 0)),
        compiler_params=_cparams(("parallel",), vmem),
        name="mixer_out",
    )(x, attn, yt, gates, w_glu, b_glu.reshape(1, sw), w_ao, w_so, w_out)


def _layer(xs, ffn1_norm, ffn1_w_gate_up, ffn1_w_down, mix_norm, w_in, q_norm, k_norm, attn_sink,
           ssm_lambda_re, ssm_lambda_im, ssm_log_dt, ssm_b_re, ssm_b_im, ssm_c_re, ssm_c_im, ssm_d,
           w_glu, b_glu, w_attn_out, w_ssm_out, w_out, ffn2_norm, ffn2_w_gate_up, ffn2_w_down):
    d = xs[0].shape[-1]
    seqs = [(x.shape[0], x.shape[1]) for x in xs]
    aw = N_Q_HEADS * HEAD_DIM
    kw = N_KV_HEADS * HEAD_DIM
    sw = ssm_d.shape[0]

    o_u, o_g = aw + 2 * kw, aw + 2 * kw + sw
    w_qkv = w_in[:, :o_u].astype(BF16)
    w_u_t = w_in[:, o_u:o_g].T.astype(BF16)
    w_gates = w_in[:, o_g:].astype(BF16)
    w1_gu, w1_d = _ffn_weights(ffn1_w_gate_up, ffn1_w_down)
    w2_gu, w2_d = _ffn_weights(ffn2_w_gate_up, ffn2_w_down)
    w_glu_b, w_ao, w_so, w_o = (w.astype(BF16) for w in (w_glu, w_attn_out, w_ssm_out, w_out))
    cos2, sin2 = _rope_tables(max(s for _, s in seqs))

    x1s, attns, gates, uts = [], [], [], []
    for x, (b, s) in zip(xs, seqs):
        x1 = _ffn(x.reshape(b * s, d), ffn1_norm, w1_gu, w1_d)
        qkv, ut, gate = _inproj(x1, mix_norm, w_qkv, w_u_t, w_gates, s, cos2, sin2, q_norm, k_norm)
        uts.append(ut)
        gates.append(gate)
        attns.append(_attention(qkv, b, s, attn_sink))
        x1s.append(x1)

    yts = _ssm(uts, seqs, ssm_lambda_re, ssm_lambda_im, ssm_log_dt, ssm_b_re, ssm_b_im, ssm_c_re, ssm_c_im, ssm_d)

    outs = []
    for x_in, x1, attn, yt, gate in zip(xs, x1s, attns, yts, gates):
        x2 = _merge(x1, attn, yt, gate, w_glu_b, b_glu, w_ao, w_so, w_o)
        outs.append(_ffn(x2, ffn2_norm, w2_gu, w2_d).reshape(x_in.shape))
    return tuple(outs)


def kernel(x_prompt, x_sample, ffn1_norm, ffn1_w_gate_up, ffn1_w_down, mix_norm, w_in, q_norm, k_norm, attn_sink, ssm_lambda_re, ssm_lambda_im, ssm_log_dt, ssm_b_re, ssm_b_im, ssm_c_re, ssm_c_im, ssm_d, w_glu, b_glu, w_attn_out, w_ssm_out, w_out, ffn2_norm, ffn2_w_gate_up, ffn2_w_down):
    xs = (x_prompt, x_sample)
    depth = ffn1_norm.shape[0]
    params = (ffn1_norm, ffn1_w_gate_up, ffn1_w_down, mix_norm, w_in, q_norm, k_norm, attn_sink,
              ssm_lambda_re, ssm_lambda_im, ssm_log_dt, ssm_b_re, ssm_b_im, ssm_c_re, ssm_c_im, ssm_d,
              w_glu, b_glu, w_attn_out, w_ssm_out, w_out, ffn2_norm, ffn2_w_gate_up, ffn2_w_down)
    for l in range(depth):
        xs = _layer(xs, *(p[l] for p in params))
    return xs
```

```python
import functools
import math

import jax
import jax.numpy as jnp
import numpy as np
from jax import lax
from jax.experimental import pallas as pl
from jax.experimental.pallas import tpu as pltpu

F32 = jnp.float32
BF16 = jnp.bfloat16

HEAD_DIM = 128
N_Q_HEADS = 8
N_KV_HEADS = 2
Q_PER_KV = N_Q_HEADS // N_KV_HEADS
WINDOW = 128
ATTN_BLOCK = 128
ROPE_THETA = 10000.0
SSM_GROUP = 16
SSM_STATE = 64
EPS = 1e-6
LOG2E = math.log2(math.e)

LANES = 128
VMEM_LIMIT_CAP = 56 * 1024 * 1024

SSM_CHUNK = 64
CHUNK_W = SSM_CHUNK * SSM_GROUP
TOK_BLOCK = 128
FFN_ROWS = 512
FFN_COLS = 512
INPROJ_ROWS = 256
MERGE_ROWS = 256
ATTN_QBLOCKS = 4

assert 2 * SSM_CHUNK == TOK_BLOCK == LANES and 2 * SSM_STATE == LANES


def _cparams(semantics, vmem_bytes):
    return pltpu.CompilerParams(dimension_semantics=semantics,
                                vmem_limit_bytes=min(int(vmem_bytes), VMEM_LIMIT_CAP))


def _rms(x, gain):
    return x * lax.rsqrt(jnp.mean(x * x, axis=-1, keepdims=True) + EPS) * gain


def _dot(a, b):
    return jnp.dot(a, b, preferred_element_type=F32)


def _dot_exact(a, b):
    return jnp.dot(a, b, preferred_element_type=F32, precision=lax.Precision.HIGHEST)


def _ffn_kernel(x_ref, gain_ref, wg_ref, wu_ref, wd_ref, o_ref, h_ref):
    @pl.when(pl.program_id(1) == 0)
    def _():
        x = x_ref[...]
        h_ref[...] = _rms(x, gain_ref[...]).astype(BF16)
        o_ref[...] = x

    h = h_ref[...]
    g = _dot(h, wg_ref[...])
    u = _dot(h, wu_ref[...])
    a = (g * jax.nn.sigmoid(g)) * u
    o_ref[...] += _dot(a.astype(BF16), wd_ref[...])


def _ffn_weights(w_gate_up, w_down):
    return w_gate_up.astype(BF16), (0.5 * w_down).astype(BF16)


def _ffn(x, gain, w_gate_up, w_down_half):
    t, d = x.shape
    d_ff = w_down_half.shape[0]
    tm, tf = FFN_ROWS, FFN_COLS
    nf = d_ff // tf
    vmem = (4 * tm * d * 4) + (tm * d * 2) + 2 * 3 * (d * tf * 2) + 4 * (tm * tf * 4) + (4 << 20)
    return pl.pallas_call(
        _ffn_kernel,
        out_shape=jax.ShapeDtypeStruct((t, d), F32),
        grid=(t // tm, nf),
        in_specs=[
            pl.BlockSpec((tm, d), lambda i, f: (i, 0)),
            pl.BlockSpec((1, d), lambda i, f: (0, 0)),
            pl.BlockSpec((d, tf), lambda i, f: (0, f)),
            pl.BlockSpec((d, tf), lambda i, f: (0, f + nf)),
            pl.BlockSpec((tf, d), lambda i, f: (f, 0)),
        ],
        out_specs=pl.BlockSpec((tm, d), lambda i, f: (i, 0)),
        scratch_shapes=[pltpu.VMEM((tm, d), BF16)],
        compiler_params=_cparams(("parallel", "arbitrary"), vmem),
        name="ffn",
    )(x, gain.reshape(1, d), w_gate_up, w_gate_up, w_down_half)


def _rope(x, cos2, sin2):
    return x * cos2 + pltpu.roll(x, HEAD_DIM // 2, 1) * sin2


def _inproj_kernel(x_ref, gain_ref, wqkv_ref, wut_ref, wg_ref, cos_ref, sin_ref, qg_ref, kg_ref,
                   qkv_ref, ut_ref, gates_ref):
    h = _rms(x_ref[...], gain_ref[...]).astype(BF16)

    r = _dot(h, wqkv_ref[...])
    cos2, sin2 = cos_ref[...], sin_ref[...]
    qg = qg_ref[...] * (LOG2E * HEAD_DIM ** -0.5)
    kg = kg_ref[...]
    n_rot = N_Q_HEADS + N_KV_HEADS
    for head in range(n_rot):
        cols = slice(head * HEAD_DIM, (head + 1) * HEAD_DIM)
        gain = qg if head < N_Q_HEADS else kg
        qkv_ref[:, cols] = _rope(_rms(r[:, cols], gain), cos2, sin2).astype(BF16)
    vcols = slice(n_rot * HEAD_DIM, (n_rot + N_KV_HEADS) * HEAD_DIM)
    qkv_ref[:, vcols] = r[:, vcols].astype(BF16)

    rt = lax.dot_general(wut_ref[...], h, (((1,), (1,)), ((), ())), preferred_element_type=F32)
    for j in range(ut_ref.shape[0]):
        ut_ref[j] = rt[:, j * TOK_BLOCK:(j + 1) * TOK_BLOCK].astype(BF16)

    gates_ref[...] = _dot(h, wg_ref[...]).astype(BF16)


def _const_spec(shape):
    return pl.BlockSpec(shape, lambda *_: (0,) * len(shape), pipeline_mode=pl.Buffered(1))


def _inproj(x, gain, w_qkv, w_u_t, w_gates, seq, cos2, sin2, q_gain, k_gain):
    t, d = x.shape
    nqkv, nu, ng = w_qkv.shape[1], w_u_t.shape[0], w_gates.shape[1]
    tm = math.gcd(INPROJ_ROWS, seq)
    tiles_per_seq = seq // tm
    n_all = nqkv + nu + ng
    vmem = d * n_all * 2 + 2 * tm * d * 4 + 2 * tm * n_all * 2 + tm * n_all * 4 + tm * d * 6 + (6 << 20)
    row = lambda n: pl.BlockSpec((tm, n), lambda i: (i, 0))
    table = pl.BlockSpec((tm, HEAD_DIM), lambda i: (i % tiles_per_seq, 0))
    return pl.pallas_call(
        _inproj_kernel,
        out_shape=(jax.ShapeDtypeStruct((t, nqkv), BF16),
                   jax.ShapeDtypeStruct((t // TOK_BLOCK, nu, TOK_BLOCK), BF16),
                   jax.ShapeDtypeStruct((t, ng), BF16)),
        grid=(t // tm,),
        in_specs=[row(d), _const_spec((1, d)), _const_spec((d, nqkv)), _const_spec((nu, d)), _const_spec((d, ng)),
                  table, table, _const_spec((1, HEAD_DIM)), _const_spec((1, HEAD_DIM))],
        out_specs=(row(nqkv), pl.BlockSpec((tm // TOK_BLOCK, nu, TOK_BLOCK), lambda i: (i, 0, 0)), row(ng)),
        compiler_params=_cparams(("parallel",), vmem),
        name="inproj",
    )(x, gain.reshape(1, d), w_qkv, w_u_t, w_gates, cos2, sin2,
      q_gain.reshape(1, HEAD_DIM), k_gain.reshape(1, HEAD_DIM))


def _rope_table_kernel(freq_ref, cos_ref, sin_ref):
    rows = cos_ref.shape[0]
    pos = (lax.broadcasted_iota(jnp.int32, (rows, HEAD_DIM), 0) + pl.program_id(0) * rows).astype(F32)
    ang = pos * freq_ref[...]
    lane = lax.broadcasted_iota(jnp.int32, (rows, HEAD_DIM), 1)
    cos_ref[...] = jnp.cos(ang)
    sin_ref[...] = jnp.where(lane < HEAD_DIM // 2, -1.0, 1.0) * jnp.sin(ang)


def _rope_tables(seq):
    inv_freq = ROPE_THETA ** (-jnp.arange(0, HEAD_DIM, 2, dtype=F32) / HEAD_DIM)
    freq2 = jnp.concatenate([inv_freq, inv_freq]).reshape(1, HEAD_DIM)
    rows = math.gcd(seq, 512)
    return pl.pallas_call(
        _rope_table_kernel,
        out_shape=(jax.ShapeDtypeStruct((seq, HEAD_DIM), F32),) * 2,
        grid=(seq // rows,),
        in_specs=[pl.BlockSpec((1, HEAD_DIM), lambda i: (0, 0))],
        out_specs=(pl.BlockSpec((rows, HEAD_DIM), lambda i: (i, 0)),) * 2,
        compiler_params=_cparams(("parallel",), 16 << 20),
        name="rope_tables",
    )(freq2)


def _attn_bias(nq):
    blk = ATTN_BLOCK
    kj = np.arange(3 * blk)[:, None]
    qi = np.arange(nq)[None, :] % blk
    ok = np.abs(kj - blk - qi) <= WINDOW
    variants = []
    for v in range(4):
        valid = ok & ((kj >= blk) | (v & 1 == 0)) & ((kj < 2 * blk) | (v & 2 == 0))
        variants.append(np.where(valid, 0.0, -np.inf))
    return jnp.asarray(np.stack(variants), F32)


def _attn_kernel(sink_ref, bias_ref, q_ref, kvp_ref, kvc_ref, kvn_ref, o_ref):
    n = pl.program_id(1)
    blk = ATTN_BLOCK
    nqb = q_ref.shape[0] // blk
    kw = N_KV_HEADS * HEAD_DIM
    no_prev = jnp.where(n == 0, 1, 0)
    no_next = jnp.where(n == pl.num_programs(1) - 1, 2, 0)

    for h in range(N_KV_HEADS):
        kc = slice(h * HEAD_DIM, (h + 1) * HEAD_DIM)
        vc = slice(kw + h * HEAD_DIM, kw + (h + 1) * HEAD_DIM)
        blocks = [kvp_ref] + [kvc_ref.at[qb * blk:(qb + 1) * blk] for qb in range(nqb)] + [kvn_ref]
        k_parts = [ref[:, kc] for ref in blocks]
        v_parts = [ref[:, vc] for ref in blocks]
        heads = range(h * Q_PER_KV, (h + 1) * Q_PER_KV)
        sink = jnp.concatenate([jnp.full((1, blk), sink_ref[hd] * LOG2E, F32) for hd in heads], axis=1)
        for qb in range(nqb):
            rows = slice(qb * blk, (qb + 1) * blk)
            variant = (no_prev if qb == 0 else 0) + (no_next if qb == nqb - 1 else 0)
            k_band = jnp.concatenate(k_parts[qb:qb + 3], axis=0)
            v_band = jnp.concatenate(v_parts[qb:qb + 3], axis=0)
            q_all = jnp.concatenate([q_ref[rows, hd * HEAD_DIM:(hd + 1) * HEAD_DIM] for hd in heads], axis=0)
            s = lax.dot_general(k_band, q_all, (((1,), (1,)), ((), ())), preferred_element_type=F32)
            s = s + bias_ref[variant]
            m = jnp.maximum(jnp.max(s, axis=0, keepdims=True), sink)
            p = jnp.exp2(s - m)
            denom = jnp.sum(p, axis=0, keepdims=True) + jnp.exp2(sink - m)
            ot = lax.dot_general(v_band, p.astype(BF16), (((0,), (0,)), ((), ())), preferred_element_type=F32)
            ot = ot / denom
            for g, hd in enumerate(heads):
                o_ref[rows, hd * HEAD_DIM:(hd + 1) * HEAD_DIM] = ot[:, g * blk:(g + 1) * blk].T.astype(BF16)


def _attention(qkv, batch, seq, sink):
    blk = ATTN_BLOCK
    nqb = math.gcd(ATTN_QBLOCKS, seq // blk)
    step = nqb * blk
    nb = seq // step
    qw = N_Q_HEADS * HEAD_DIM
    kvw = 2 * N_KV_HEADS * HEAD_DIM
    assert qw % kvw == 0 and seq % blk == 0
    kv_col = qw // kvw
    nq = Q_PER_KV * blk
    grid_spec = pltpu.PrefetchScalarGridSpec(
        num_scalar_prefetch=1,
        grid=(batch, nb),
        in_specs=[
            _const_spec((4, 3 * blk, nq)),
            pl.BlockSpec((step, qw), lambda b, n, s: (b * nb + n, 0)),
            pl.BlockSpec((blk, kvw), lambda b, n, s: (nqb * (b * nb) + jnp.maximum(nqb * n - 1, 0), kv_col)),
            pl.BlockSpec((step, kvw), lambda b, n, s: (b * nb + n, kv_col)),
            pl.BlockSpec((blk, kvw),
                         lambda b, n, s: (nqb * (b * nb) + jnp.minimum(nqb * n + nqb, nqb * nb - 1), kv_col)),
        ],
        out_specs=pl.BlockSpec((step, qw), lambda b, n, s: (b * nb + n, 0)),
    )
    return pl.pallas_call(
        _attn_kernel,
        out_shape=jax.ShapeDtypeStruct((batch * seq, qw), BF16),
        grid_spec=grid_spec,
        compiler_params=_cparams(("parallel", "parallel"), 40 << 20),
        name="attention",
    )(sink, _attn_bias(nq), qkv, qkv, qkv, qkv)


def _cmul(ar, ai, br, bi):
    return ar * br - ai * bi, ar * bi + ai * br


def _cpow_table(br, bi, expo, nbits):
    pr = jnp.where((expo & 1) == 1, br, 1.0)
    pi = jnp.where((expo & 1) == 1, bi, 0.0)
    for k in range(1, nbits):
        br, bi = _cmul(br, bi, br, bi)
        bit = ((expo >> k) & 1) == 1
        nr, ni = _cmul(pr, pi, br, bi)
        pr = jnp.where(bit, nr, pr)
        pi = jnp.where(bit, ni, pi)
    return pr, pi


def _lambda_bar(lre, lim, log_dt):
    dt = jnp.exp(log_dt)
    mag = jnp.exp(lre * dt)
    return mag * jnp.cos(lim * dt), mag * jnp.sin(lim * dt)


def _ssm_weights(rowp_ref, colp_ref, btr_ref, bti_ref, crr_ref, cri_ref, ctr_ref, cti_ref, dcol_ref,
                 wz_s, wy_s, g_s):
    L, P, C = SSM_CHUNK, SSM_STATE, SSM_GROUP
    nbits = L.bit_length()

    lre, lim, ldt = rowp_ref[0, 0:1, :], rowp_ref[0, 1:2, :], rowp_ref[0, 2:3, :]
    lbr, lbi = _lambda_bar(lre, lim, ldt)
    den = lre * lre + lim * lim
    cfr = ((lbr - 1.0) * lre + lbi * lim) / den
    cfi = (lbi * lre - (lbr - 1.0) * lim) / den
    bbr, bbi = _cmul(cfr, cfi, btr_ref[0], bti_ref[0])

    dr, di = lbr, lbi
    for _ in range(L.bit_length() - 1):
        dr, di = _cmul(dr, di, dr, di)

    n_sub = lax.broadcasted_iota(jnp.int32, (L, 2 * P), 0)
    lane = lax.broadcasted_iota(jnp.int32, (L, 2 * P), 1)
    pzr, pzi = _cpow_table(lbr, lbi, jnp.where(lane < P, L - 1 - n_sub, n_sub), nbits)
    for c in range(C):
        zr, zi = _cmul(pzr, pzi, bbr[c:c + 1], bbi[c:c + 1])
        wz_s[c * L:(c + 1) * L, 0:2 * P] = zr.astype(BF16)
        wz_s[c * L:(c + 1) * L, 2 * P:4 * P] = zi.astype(BF16)

    b_rep_r = jnp.concatenate([jnp.broadcast_to(bbr[c:c + 1], (C, 2 * P)) for c in range(C)], axis=0)
    b_rep_i = jnp.concatenate([jnp.broadcast_to(bbi[c:c + 1], (C, 2 * P)) for c in range(C)], axis=0)
    c_tile_r = jnp.concatenate([crr_ref[0]] * C, axis=0)
    c_tile_i = jnp.concatenate([cri_ref[0]] * C, axis=0)
    bcr, bci = _cmul(b_rep_r, b_rep_i, c_tile_r, c_tile_i)

    clre, clim, cldt = colp_ref[0, :, 0:1], colp_ref[0, :, 1:2], colp_ref[0, :, 2:3]
    cbr, cbi = _lambda_bar(clre, clim, cldt)
    sub = lax.broadcasted_iota(jnp.int32, (2 * P, LANES), 0)
    m_lane = lax.broadcasted_iota(jnp.int32, (2 * P, LANES), 1)

    pgr, pgi = _cpow_table(cbr, cbi, jnp.abs(m_lane - (L - 1)), nbits)
    live = ((sub < P) & (m_lane >= L - 1)) | ((sub >= P) & (m_lane <= L - 1))
    pgr = jnp.where(live, pgr, 0.0)
    pgi = jnp.where(live, pgi, 0.0)
    g = _dot_exact(bcr, pgr) - _dot_exact(bci, pgi)
    g_row = lax.broadcasted_iota(jnp.int32, (C * C, LANES), 0)
    g_lane = lax.broadcasted_iota(jnp.int32, (C * C, LANES), 1)
    log2c = C.bit_length() - 1
    skip = ((g_row >> log2c) == (g_row & (C - 1))) & (g_lane == L - 1)
    g_s[...] = g + jnp.where(skip, dcol_ref[0], 0.0)

    j_lane = m_lane & (L - 1)
    pyr, pyi = _cpow_table(cbr, cbi, jnp.where(sub < P, j_lane + 1, L - j_lane), nbits)
    ctr, cti = ctr_ref[0], cti_ref[0]
    low = m_lane < L
    for a in range(C // 2):
        cr = jnp.where(low, ctr[:, 2 * a:2 * a + 1], ctr[:, 2 * a + 1:2 * a + 2])
        ci = jnp.where(low, cti[:, 2 * a:2 * a + 1], cti[:, 2 * a + 1:2 * a + 2])
        yr, yi = _cmul(cr, ci, pyr, pyi)
        wy_s[0:2 * P, a * LANES:(a + 1) * LANES] = yr.astype(BF16)
        wy_s[2 * P:4 * P, a * LANES:(a + 1) * LANES] = (-yi).astype(BF16)

    return dr, di


def _fill_toeplitz(g_s, t_s, c_ins):
    L, C = SSM_CHUNK, SSM_GROUP
    t_low = lax.broadcasted_iota(jnp.int32, (L, LANES), 1) < L
    for c_in in c_ins:
        for a in range(C // 2):
            row = c_in * C + 2 * a
            ge = jnp.broadcast_to(g_s[row:row + 1, :], (L, LANES))
            go = jnp.broadcast_to(g_s[row + 1:row + 2, :], (L, LANES))
            te = pltpu.roll(ge, L + 1, 1, stride=1, stride_axis=0)
            to = pltpu.roll(go, 1, 1, stride=1, stride_axis=0)
            t_s[c_in * L:(c_in + 1) * L, a * LANES:(a + 1) * LANES] = jnp.where(t_low, te, to).astype(BF16)


def _ssm_kernel(*refs, seqs):
    P, L, C = SSM_STATE, SSM_CHUNK, SSM_GROUP
    ns = len(seqs)
    u_refs, refs = refs[:ns], refs[ns:]
    param_refs, refs = refs[:9], refs[9:]
    y_refs, refs = refs[:ns], refs[ns:]
    t_s, wz_s, wy_s, g_s, a_s, yt_s, zr_s, zi_s, wr_s, wi_s, fr_s, fi_s, rr_s, ri_s = refs
    blocks = [nseq * (slen // TOK_BLOCK) for nseq, slen in seqs]
    nblk = sum(blocks)
    ar, ai = _ssm_weights(*param_refs, wz_s, wy_s, g_s)
    _fill_toeplitz(g_s, t_s, range(0, C // 2))

    blk0 = 0
    for u_ref, nb in zip(u_refs, blocks):
        a_s[blk0 * C:(blk0 + nb) * C, :] = u_ref[...].astype(F32).reshape(nb * C, TOK_BLOCK)
        blk0 += nb

    src = lax.broadcasted_iota(jnp.int32, (2 * LANES, 2 * LANES), 0)
    dst = lax.broadcasted_iota(jnp.int32, (2 * LANES, 2 * LANES), 1)
    swap = ((src & (L - 1)) | ((src & L) << 1) | ((src & LANES) >> 1)) == dst
    perm = jnp.where(swap, 1.0, 0.0).astype(BF16)
    first, second = [], []
    for a in range(C // 2):
        x01 = jnp.concatenate([a_s[pl.ds(2 * a, nblk, stride=C), :],
                               a_s[pl.ds(2 * a + 1, nblk, stride=C), :]], axis=1).astype(BF16)
        halves = _dot(x01, perm).astype(BF16)
        first.append(halves[:, 0:LANES])
        second.append(halves[:, LANES:2 * LANES])
    u = jnp.concatenate([jnp.concatenate(first, axis=1), jnp.concatenate(second, axis=1)], axis=0)

    z = _dot(u, wz_s[...])
    zr_s[...] = z[:, 0:2 * P]
    zi_s[...] = z[:, 2 * P:4 * P]

    fwd = lax.broadcasted_iota(jnp.int32, (nblk, 2 * P), 1) < P
    z1r, z1i = z[0:nblk, 0:2 * P], z[0:nblk, 2 * P:4 * P]
    z2r, z2i = z[nblk:2 * nblk, 0:2 * P], z[nblk:2 * nblk, 2 * P:4 * P]
    wr_s[...] = jnp.where(fwd, ar * z1r - ai * z1i + z2r, ar * z2r - ai * z2i + z1r)
    wi_s[...] = jnp.where(fwd, ar * z1i + ai * z1r + z2i, ar * z2i + ai * z2r + z1i)
    a2r, a2i = _cmul(ar, ai, ar, ai)

    def scan(blk0, nseq, n):
        is_fwd = lax.broadcasted_iota(jnp.int32, (nseq, 2 * P), 1) < P

        def step(t, carry):
            sr, si = carry
            f_rows = pl.ds(blk0 + t, nseq, stride=n)
            b_rows = pl.ds(blk0 + n - 1 - t, nseq, stride=n)
            fr_s[f_rows, :] = sr
            fi_s[f_rows, :] = si
            rr_s[b_rows, :] = sr
            ri_s[b_rows, :] = si
            wr = jnp.where(is_fwd, wr_s[f_rows, :], wr_s[b_rows, :])
            wi = jnp.where(is_fwd, wi_s[f_rows, :], wi_s[b_rows, :])
            return a2r * sr - a2i * si + wr, a2r * si + a2i * sr + wi

        zero = jnp.zeros((nseq, 2 * P), F32)
        lax.fori_loop(0, n, step, (zero, zero))

    blk0 = 0
    for (nseq, slen), nb in zip(seqs, blocks):
        scan(blk0, nseq, slen // TOK_BLOCK)
        blk0 += nb

    fr, fi, rr, ri = fr_s[...], fi_s[...], rr_s[...], ri_s[...]
    z1r, z1i, z2r, z2i = zr_s[0:nblk, :], zi_s[0:nblk, :], zr_s[nblk:2 * nblk, :], zi_s[nblk:2 * nblk, :]
    s1r = jnp.where(fwd, fr, ar * rr - ai * ri + z2r)
    s1i = jnp.where(fwd, fi, ar * ri + ai * rr + z2i)
    s2r = jnp.where(fwd, ar * fr - ai * fi + z1r, rr)
    s2i = jnp.where(fwd, ar * fi + ai * fr + z1i, ri)
    s_in = jnp.concatenate([jnp.concatenate([s1r, s2r], axis=0),
                            jnp.concatenate([s1i, s2i], axis=0)], axis=1).astype(BF16)
    _fill_toeplitz(g_s, t_s, range(C // 2, C))
    half = CHUNK_W // 2
    y = (_dot(u[:, :half], t_s[0:half, :]) + _dot(s_in, wy_s[...])
         + _dot(u[:, half:], t_s[half:CHUNK_W, :]))

    for a in range(C // 2):
        y01 = jnp.concatenate([y[0:nblk, a * LANES:(a + 1) * LANES],
                               y[nblk:2 * nblk, a * LANES:(a + 1) * LANES]], axis=1).astype(BF16)
        chans = _dot(y01, perm)
        yt_s[pl.ds(2 * a, nblk, stride=C), :] = chans[:, 0:LANES]
        yt_s[pl.ds(2 * a + 1, nblk, stride=C), :] = chans[:, LANES:2 * LANES]
    blk0 = 0
    for y_ref, nb in zip(y_refs, blocks):
        y_ref[...] = yt_s[blk0 * C:(blk0 + nb) * C, :].reshape(nb, C, TOK_BLOCK).astype(y_ref.dtype)
        blk0 += nb


def _ssm(uts, seqs, lam_re, lam_im, log_dt, b_re, b_im, c_re, c_im, d_skip):
    width = uts[0].shape[1]
    _, groups, P = lam_re.shape
    C = SSM_GROUP
    assert P == SSM_STATE and b_re.shape[-1] == C and groups * C == width
    assert all(s % TOK_BLOCK == 0 for _, s in seqs)
    assert all(ut.shape == (b * s // TOK_BLOCK, width, TOK_BLOCK) for ut, (b, s) in zip(uts, seqs))
    nblk = sum(ut.shape[0] for ut in uts)

    def lanes_fb(a):
        return jnp.transpose(a, (1, 0, 2)).reshape(groups, 2 * P)

    ldt = jnp.broadcast_to(log_dt[:, :, None], (2, groups, P))
    rowp = jnp.stack([lanes_fb(lam_re), lanes_fb(lam_im), lanes_fb(ldt)], axis=1)
    colp = jnp.transpose(rowp, (0, 2, 1))
    bt = lambda b: jnp.transpose(b, (1, 3, 0, 2)).reshape(groups, C, 2 * P)
    cr = lambda c: jnp.transpose(c, (1, 2, 0, 3)).reshape(groups, C, 2 * P)
    ct = lambda c: jnp.transpose(c, (1, 0, 3, 2)).reshape(groups, 2 * P, C)
    dcol = jnp.tile(d_skip.reshape(groups, 1, C), (1, C, 1)).reshape(groups, C * C, 1)

    g3 = lambda s1, s2: pl.BlockSpec((1, s1, s2), lambda g: (g, 0, 0))
    acts = [pl.BlockSpec((ut.shape[0], C, TOK_BLOCK), lambda g: (0, g, 0)) for ut in uts]
    return pl.pallas_call(
        functools.partial(_ssm_kernel, seqs=tuple(seqs)),
        out_shape=[jax.ShapeDtypeStruct(ut.shape, BF16) for ut in uts],
        grid=(groups,),
        in_specs=acts + [g3(3, 2 * P), g3(2 * P, 3), g3(C, 2 * P), g3(C, 2 * P), g3(C, 2 * P), g3(C, 2 * P),
                         g3(2 * P, C), g3(2 * P, C), g3(C * C, 1)],
        out_specs=acts,
        scratch_shapes=[
            pltpu.VMEM((CHUNK_W, CHUNK_W), BF16),
            pltpu.VMEM((CHUNK_W, 4 * P), BF16),
            pltpu.VMEM((4 * P, CHUNK_W), BF16),
            pltpu.VMEM((C * C, LANES), F32),
            pltpu.VMEM((nblk * C, TOK_BLOCK), F32),
            pltpu.VMEM((nblk * C, TOK_BLOCK), F32),
        ] + [pltpu.VMEM((2 * nblk, 2 * P), F32)] * 2 + [pltpu.VMEM((nblk, 2 * P), F32)] * 6,
        compiler_params=_cparams(("parallel",), 40 << 20),
        name="ssm",
    )(*uts, rowp, colp, bt(b_re), bt(b_im), cr(c_re), cr(c_im), ct(c_re), ct(c_im), dcol)


def _merge_kernel(x_ref, a_ref, yt_ref, gate_ref, wglu_ref, bglu_ref, wao_ref, wso_ref, wout_ref, o_ref):
    d = x_ref.shape[1]
    y = jnp.concatenate([yt_ref[j].astype(F32).T for j in range(yt_ref.shape[0])], axis=0)
    z = jax.nn.gelu(y)
    z = z * jax.nn.sigmoid(_dot(z.astype(BF16), wglu_ref[...]) + bglu_ref[...])
    m = _dot(z.astype(BF16), wso_ref[...])
    a = _dot(a_ref[...], wao_ref[...])
    g_attn, g_ssm = gate_ref[:, 0:d].astype(F32), gate_ref[:, d:2 * d].astype(F32)
    merged = jax.nn.sigmoid(g_attn) * a + jax.nn.sigmoid(g_ssm) * m
    o_ref[...] = x_ref[...] + _dot(merged.astype(BF16), wout_ref[...])


def _merge(x, attn, yt, gates, w_glu, b_glu, w_ao, w_so, w_out):
    t, d = x.shape
    aw, sw = attn.shape[1], yt.shape[1]
    tm = MERGE_ROWS
    weights = (sw * sw + aw * d + sw * d + d * d) * 2
    vmem = weights + 2 * tm * (2 * d * 4 + (aw + sw + 2 * d) * 2) + 6 * tm * d * 4 + (4 << 20)
    return pl.pallas_call(
        _merge_kernel,
        out_shape=jax.ShapeDtypeStruct((t, d), F32),
        grid=(t // tm,),
        in_specs=[
            pl.BlockSpec((tm, d), lambda i: (i, 0)),
            pl.BlockSpec((tm, aw), lambda i: (i, 0)),
            pl.BlockSpec((tm // TOK_BLOCK, sw, TOK_BLOCK), lambda i: (i, 0, 0)),
            pl.BlockSpec((tm, 2 * d), lambda i: (i, 0)),
            _const_spec((sw, sw)), _const_spec((1, sw)), _const_spec((aw, d)), _const_spec((sw, d)),
            _const_spec((d, d)),
        ],
        out_specs=pl.BlockSpec((tm, d), lambda i: (i, 0)),
        compiler_params=_cparams(("parallel",), vmem),
        name="mixer_out",
    )(x, attn, yt, gates, w_glu, b_glu.reshape(1, sw), w_ao, w_so, w_out)


def _layer(xs, ffn1_norm, ffn1_w_gate_up, ffn1_w_down, mix_norm, w_in, q_norm, k_norm, attn_sink,
           ssm_lambda_re, ssm_lambda_im, ssm_log_dt, ssm_b_re, ssm_b_im, ssm_c_re, ssm_c_im, ssm_d,
           w_glu, b_glu, w_attn_out, w_ssm_out, w_out, ffn2_norm, ffn2_w_gate_up, ffn2_w_down):
    d = xs[0].shape[-1]
    seqs = [(x.shape[0], x.shape[1]) for x in xs]
    aw = N_Q_HEADS * HEAD_DIM
    kw = N_KV_HEADS * HEAD_DIM
    sw = ssm_d.shape[0]

    o_u, o_g = aw + 2 * kw, aw + 2 * kw + sw
    w_qkv = w_in[:, :o_u].astype(BF16)
    w_u_t = w_in[:, o_u:o_g].T.astype(BF16)
    w_gates = w_in[:, o_g:].astype(BF16)
    w1_gu, w1_d = _ffn_weights(ffn1_w_gate_up, ffn1_w_down)
    w2_gu, w2_d = _ffn_weights(ffn2_w_gate_up, ffn2_w_down)
    w_glu_b, w_ao, w_so, w_o = (w.astype(BF16) for w in (w_glu, w_attn_out, w_ssm_out, w_out))
    cos2, sin2 = _rope_tables(max(s for _, s in seqs))

    x1s, attns, gates, uts = [], [], [], []
    for x, (b, s) in zip(xs, seqs):
        x1 = _ffn(x.reshape(b * s, d), ffn1_norm, w1_gu, w1_d)
        qkv, ut, gate = _inproj(x1, mix_norm, w_qkv, w_u_t, w_gates, s, cos2, sin2, q_norm, k_norm)
        uts.append(ut)
        gates.append(gate)
        attns.append(_attention(qkv, b, s, attn_sink))
        x1s.append(x1)

    yts = _ssm(uts, seqs, ssm_lambda_re, ssm_lambda_im, ssm_log_dt, ssm_b_re, ssm_b_im, ssm_c_re, ssm_c_im, ssm_d)

    outs = []
    for x_in, x1, attn, yt, gate in zip(xs, x1s, attns, yts, gates):
        x2 = _merge(x1, attn, yt, gate, w_glu_b, b_glu, w_ao, w_so, w_o)
        outs.append(_ffn(x2, ffn2_norm, w2_gu, w2_d).reshape(x_in.shape))
    return tuple(outs)


def kernel(x_prompt, x_sample, ffn1_norm, ffn1_w_gate_up, ffn1_w_down, mix_norm, w_in, q_norm, k_norm, attn_sink, ssm_lambda_re, ssm_lambda_im, ssm_log_dt, ssm_b_re, ssm_b_im, ssm_c_re, ssm_c_im, ssm_d, w_glu, b_glu, w_attn_out, w_ssm_out, w_out, ffn2_norm, ffn2_w_gate_up, ffn2_w_down):
    xs = (x_prompt, x_sample)
    depth = ffn1_norm.shape[0]
    params = (ffn1_norm, ffn1_w_gate_up, ffn1_w_down, mix_norm, w_in, q_norm, k_norm, attn_sink,
              ssm_lambda_re, ssm_lambda_im, ssm_log_dt, ssm_b_re, ssm_b_im, ssm_c_re, ssm_c_im, ssm_d,
              w_glu, b_glu, w_attn_out, w_ssm_out, w_out, ffn2_norm, ffn2_w_gate_up, ffn2_w_down)
    for l in range(depth):
        xs = _layer(xs, *(p[l] for p in params))
    return xs
```

```python
import functools
import math

import jax
import jax.numpy as jnp
import numpy as np
from jax import lax
from jax.experimental import pallas as pl
from jax.experimental.pallas import tpu as pltpu

F32 = jnp.float32
BF16 = jnp.bfloat16

HEAD_DIM = 128
N_Q_HEADS = 8
N_KV_HEADS = 2
Q_PER_KV = N_Q_HEADS // N_KV_HEADS
WINDOW = 128
ATTN_BLOCK = 128
ROPE_THETA = 10000.0
SSM_GROUP = 16
SSM_STATE = 64
EPS = 1e-6
LOG2E = math.log2(math.e)

LANES = 128
VMEM_LIMIT_CAP = 56 * 1024 * 1024

SSM_CHUNK = 64
CHUNK_W = SSM_CHUNK * SSM_GROUP
TOK_BLOCK = 128
FFN_ROWS = 512
FFN_COLS = 512
INPROJ_ROWS = 256
MERGE_ROWS = 256
ATTN_QBLOCKS = 8

assert 2 * SSM_CHUNK == TOK_BLOCK == LANES and 2 * SSM_STATE == LANES


def _cparams(semantics, vmem_bytes):
    return pltpu.CompilerParams(dimension_semantics=semantics,
                                vmem_limit_bytes=min(int(vmem_bytes), VMEM_LIMIT_CAP))


def _rms(x, gain):
    return x * lax.rsqrt(jnp.mean(x * x, axis=-1, keepdims=True) + EPS) * gain


def _dot(a, b):
    return jnp.dot(a, b, preferred_element_type=F32)


def _dot_exact(a, b):
    return jnp.dot(a, b, preferred_element_type=F32, precision=lax.Precision.HIGHEST)


def _ffn_kernel(x_ref, gain_ref, wg_ref, wu_ref, wd_ref, o_ref, h_ref):
    @pl.when(pl.program_id(1) == 0)
    def _():
        x = x_ref[...]
        h_ref[...] = _rms(x, gain_ref[...]).astype(BF16)
        o_ref[...] = x

    h = h_ref[...]
    g = _dot(h, wg_ref[...])
    u = _dot(h, wu_ref[...])
    a = (g * jax.nn.sigmoid(g)) * u
    o_ref[...] += _dot(a.astype(BF16), wd_ref[...])


def _ffn_weights(w_gate_up, w_down):
    return w_gate_up.astype(BF16), (0.5 * w_down).astype(BF16)


def _ffn(x, gain, w_gate_up, w_down_half):
    t, d = x.shape
    d_ff = w_down_half.shape[0]
    tm, tf = FFN_ROWS, FFN_COLS
    nf = d_ff // tf
    vmem = (4 * tm * d * 4) + (tm * d * 2) + 2 * 3 * (d * tf * 2) + 4 * (tm * tf * 4) + (4 << 20)
    return pl.pallas_call(
        _ffn_kernel,
        out_shape=jax.ShapeDtypeStruct((t, d), F32),
        grid=(t // tm, nf),
        in_specs=[
            pl.BlockSpec((tm, d), lambda i, f: (i, 0)),
            pl.BlockSpec((1, d), lambda i, f: (0, 0)),
            pl.BlockSpec((d, tf), lambda i, f: (0, f)),
            pl.BlockSpec((d, tf), lambda i, f: (0, f + nf)),
            pl.BlockSpec((tf, d), lambda i, f: (f, 0)),
        ],
        out_specs=pl.BlockSpec((tm, d), lambda i, f: (i, 0)),
        scratch_shapes=[pltpu.VMEM((tm, d), BF16)],
        compiler_params=_cparams(("parallel", "arbitrary"), vmem),
        name="ffn",
    )(x, gain.reshape(1, d), w_gate_up, w_gate_up, w_down_half)


def _rope(x, cos2, sin2):
    return x * cos2 + pltpu.roll(x, HEAD_DIM // 2, 1) * sin2


def _inproj_kernel(x_ref, gain_ref, wqkv_ref, wut_ref, wg_ref, rope_ref, qg_ref, kg_ref,
                   qkv_ref, ut_ref, gates_ref):
    h = _rms(x_ref[...], gain_ref[...]).astype(BF16)

    r = _dot(h, wqkv_ref[...])
    cos2, sin2 = rope_ref[:, 0:HEAD_DIM], rope_ref[:, HEAD_DIM:2 * HEAD_DIM]
    qg = qg_ref[...] * (LOG2E * HEAD_DIM ** -0.5)
    kg = kg_ref[...]
    n_rot = N_Q_HEADS + N_KV_HEADS
    for head in range(n_rot):
        cols = slice(head * HEAD_DIM, (head + 1) * HEAD_DIM)
        gain = qg if head < N_Q_HEADS else kg
        qkv_ref[:, cols] = _rope(_rms(r[:, cols], gain), cos2, sin2).astype(BF16)
    vcols = slice(n_rot * HEAD_DIM, (n_rot + N_KV_HEADS) * HEAD_DIM)
    qkv_ref[:, vcols] = r[:, vcols].astype(BF16)

    rt = lax.dot_general(wut_ref[...], h, (((1,), (1,)), ((), ())), preferred_element_type=F32)
    for j in range(ut_ref.shape[0]):
        ut_ref[j] = rt[:, j * TOK_BLOCK:(j + 1) * TOK_BLOCK].astype(BF16)

    gates_ref[...] = _dot(h, wg_ref[...]).astype(BF16)


def _const_spec(shape):
    return pl.BlockSpec(shape, lambda *_: (0,) * len(shape), pipeline_mode=pl.Buffered(1))


def _inproj(x, gain, w_qkv, w_u_t, w_gates, seq, rope_tab, q_gain, k_gain):
    t, d = x.shape
    nqkv, nu, ng = w_qkv.shape[1], w_u_t.shape[0], w_gates.shape[1]
    tm = math.gcd(INPROJ_ROWS, seq)
    tiles_per_seq = seq // tm
    n_all = nqkv + nu + ng
    vmem = d * n_all * 2 + 2 * tm * d * 4 + 2 * tm * n_all * 2 + tm * n_all * 4 + tm * d * 6 + (6 << 20)
    row = lambda n: pl.BlockSpec((tm, n), lambda i: (i, 0))
    table = pl.BlockSpec((tm, 2 * HEAD_DIM), lambda i: (i % tiles_per_seq, 0))
    return pl.pallas_call(
        _inproj_kernel,
        out_shape=(jax.ShapeDtypeStruct((t, nqkv), BF16),
                   jax.ShapeDtypeStruct((t // TOK_BLOCK, nu, TOK_BLOCK), BF16),
                   jax.ShapeDtypeStruct((t, ng), BF16)),
        grid=(t // tm,),
        in_specs=[row(d), _const_spec((1, d)), _const_spec((d, nqkv)), _const_spec((nu, d)), _const_spec((d, ng)),
                  table, _const_spec((1, HEAD_DIM)), _const_spec((1, HEAD_DIM))],
        out_specs=(row(nqkv), pl.BlockSpec((tm // TOK_BLOCK, nu, TOK_BLOCK), lambda i: (i, 0, 0)), row(ng)),
        compiler_params=_cparams(("parallel",), vmem),
        name="inproj",
    )(x, gain.reshape(1, d), w_qkv, w_u_t, w_gates, rope_tab,
      q_gain.reshape(1, HEAD_DIM), k_gain.reshape(1, HEAD_DIM))


def _rope_table_kernel(freq_ref, tab_ref):
    rows = tab_ref.shape[0]
    pos = (lax.broadcasted_iota(jnp.int32, (rows, HEAD_DIM), 0) + pl.program_id(0) * rows).astype(F32)
    ang = pos * freq_ref[...]
    lane = lax.broadcasted_iota(jnp.int32, (rows, HEAD_DIM), 1)
    tab_ref[:, 0:HEAD_DIM] = jnp.cos(ang)
    tab_ref[:, HEAD_DIM:2 * HEAD_DIM] = jnp.where(lane < HEAD_DIM // 2, -1.0, 1.0) * jnp.sin(ang)


def _rope_table(seq):
    inv_freq = ROPE_THETA ** (-jnp.arange(0, HEAD_DIM, 2, dtype=F32) / HEAD_DIM)
    freq2 = jnp.concatenate([inv_freq, inv_freq]).reshape(1, HEAD_DIM)
    rows = math.gcd(seq, 512)
    return pl.pallas_call(
        _rope_table_kernel,
        out_shape=jax.ShapeDtypeStruct((seq, 2 * HEAD_DIM), F32),
        grid=(seq // rows,),
        in_specs=[pl.BlockSpec((1, HEAD_DIM), lambda i: (0, 0))],
        out_specs=pl.BlockSpec((rows, 2 * HEAD_DIM), lambda i: (i, 0)),
        compiler_params=_cparams(("parallel",), 16 << 20),
        name="rope_table",
    )(freq2)


def _attn_bias(nq):
    blk = ATTN_BLOCK
    kj = np.arange(3 * blk)[:, None]
    qi = np.arange(nq)[None, :] % blk
    ok = np.abs(kj - blk - qi) <= WINDOW
    variants = []
    for v in range(4):
        valid = ok & ((kj >= blk) | (v & 1 == 0)) & ((kj < 2 * blk) | (v & 2 == 0))
        variants.append(np.where(valid, 0.0, -np.inf))
    return jnp.asarray(np.stack(variants), F32)


def _attn_kernel(sink_ref, bias_ref, q_ref, kvp_ref, kvc_ref, kvn_ref, o_ref):
    n = pl.program_id(1)
    blk = ATTN_BLOCK
    nqb = q_ref.shape[0] // blk
    kw = N_KV_HEADS * HEAD_DIM
    no_prev = jnp.where(n == 0, 1, 0)
    no_next = jnp.where(n == pl.num_programs(1) - 1, 2, 0)

    for h in range(N_KV_HEADS):
        kc = slice(h * HEAD_DIM, (h + 1) * HEAD_DIM)
        vc = slice(kw + h * HEAD_DIM, kw + (h + 1) * HEAD_DIM)
        blocks = [kvp_ref] + [kvc_ref.at[qb * blk:(qb + 1) * blk] for qb in range(nqb)] + [kvn_ref]
        k_parts = [ref[:, kc] for ref in blocks]
        v_parts = [ref[:, vc] for ref in blocks]
        heads = range(h * Q_PER_KV, (h + 1) * Q_PER_KV)
        sink = jnp.concatenate([jnp.full((1, blk), sink_ref[hd] * LOG2E, F32) for hd in heads], axis=1)
        for qb in range(nqb):
            rows = slice(qb * blk, (qb + 1) * blk)
            variant = (no_prev if qb == 0 else 0) + (no_next if qb == nqb - 1 else 0)
            k_band = jnp.concatenate(k_parts[qb:qb + 3], axis=0)
            v_band = jnp.concatenate(v_parts[qb:qb + 3], axis=0)
            q_all = jnp.concatenate([q_ref[rows, hd * HEAD_DIM:(hd + 1) * HEAD_DIM] for hd in heads], axis=0)
            s = lax.dot_general(k_band, q_all, (((1,), (1,)), ((), ())), preferred_element_type=F32)
            s = s + bias_ref[variant]
            m = jnp.maximum(jnp.max(s, axis=0, keepdims=True), sink)
            p = jnp.exp2(s - m)
            denom = jnp.sum(p, axis=0, keepdims=True) + jnp.exp2(sink - m)
            ot = lax.dot_general(v_band, p.astype(BF16), (((0,), (0,)), ((), ())), preferred_element_type=F32)
            ot = ot / denom
            for g, hd in enumerate(heads):
                o_ref[rows, hd * HEAD_DIM:(hd + 1) * HEAD_DIM] = ot[:, g * blk:(g + 1) * blk].T.astype(BF16)


def _attention(qkv, batch, seq, sink):
    blk = ATTN_BLOCK
    nqb = math.gcd(ATTN_QBLOCKS, seq // blk)
    step = nqb * blk
    nb = seq // step
    qw = N_Q_HEADS * HEAD_DIM
    kvw = 2 * N_KV_HEADS * HEAD_DIM
    assert qw % kvw == 0 and seq % blk == 0
    kv_col = qw // kvw
    nq = Q_PER_KV * blk
    grid_spec = pltpu.PrefetchScalarGridSpec(
        num_scalar_prefetch=1,
        grid=(batch, nb),
        in_specs=[
            _const_spec((4, 3 * blk, nq)),
            pl.BlockSpec((step, qw), lambda b, n, s: (b * nb + n, 0)),
            pl.BlockSpec((blk, kvw), lambda b, n, s: (nqb * (b * nb) + jnp.maximum(nqb * n - 1, 0), kv_col)),
            pl.BlockSpec((step, kvw), lambda b, n, s: (b * nb + n, kv_col)),
            pl.BlockSpec((blk, kvw),
                         lambda b, n, s: (nqb * (b * nb) + jnp.minimum(nqb * n + nqb, nqb * nb - 1), kv_col)),
        ],
        out_specs=pl.BlockSpec((step, qw), lambda b, n, s: (b * nb + n, 0)),
    )
    return pl.pallas_call(
        _attn_kernel,
        out_shape=jax.ShapeDtypeStruct((batch * seq, qw), BF16),
        grid_spec=grid_spec,
        compiler_params=_cparams(("parallel", "parallel"), 40 << 20),
        name="attention",
    )(sink, _attn_bias(nq), qkv, qkv, qkv, qkv)


def _cmul(ar, ai, br, bi):
    return ar * br - ai * bi, ar * bi + ai * br


def _cpow_table(br, bi, expo, nbits):
    pr = jnp.where((expo & 1) == 1, br, 1.0)
    pi = jnp.where((expo & 1) == 1, bi, 0.0)
    for k in range(1, nbits):
        br, bi = _cmul(br, bi, br, bi)
        bit = ((expo >> k) & 1) == 1
        nr, ni = _cmul(pr, pi, br, bi)
        pr = jnp.where(bit, nr, pr)
        pi = jnp.where(bit, ni, pi)
    return pr, pi


def _lambda_bar(lre, lim, log_dt):
    dt = jnp.exp(log_dt)
    mag = jnp.exp(lre * dt)
    return mag * jnp.cos(lim * dt), mag * jnp.sin(lim * dt)


def _ssm_weights(rowp_ref, colp_ref, btr_ref, bti_ref, crr_ref, cri_ref, ctr_ref, cti_ref, dcol_ref,
                 wz_s, wy_s, g_s):
    L, P, C = SSM_CHUNK, SSM_STATE, SSM_GROUP
    nbits = L.bit_length()

    lre, lim, ldt = rowp_ref[0, 0:1, :], rowp_ref[0, 1:2, :], rowp_ref[0, 2:3, :]
    lbr, lbi = _lambda_bar(lre, lim, ldt)
    den = lre * lre + lim * lim
    cfr = ((lbr - 1.0) * lre + lbi * lim) / den
    cfi = (lbi * lre - (lbr - 1.0) * lim) / den
    bbr, bbi = _cmul(cfr, cfi, btr_ref[0], bti_ref[0])

    dr, di = lbr, lbi
    for _ in range(L.bit_length() - 1):
        dr, di = _cmul(dr, di, dr, di)

    n_sub = lax.broadcasted_iota(jnp.int32, (L, 2 * P), 0)
    lane = lax.broadcasted_iota(jnp.int32, (L, 2 * P), 1)
    pzr, pzi = _cpow_table(lbr, lbi, jnp.where(lane < P, L - 1 - n_sub, n_sub), nbits)
    for c in range(C):
        zr, zi = _cmul(pzr, pzi, bbr[c:c + 1], bbi[c:c + 1])
        wz_s[c * L:(c + 1) * L, 0:2 * P] = zr.astype(BF16)
        wz_s[c * L:(c + 1) * L, 2 * P:4 * P] = zi.astype(BF16)

    b_rep_r = jnp.concatenate([jnp.broadcast_to(bbr[c:c + 1], (C, 2 * P)) for c in range(C)], axis=0)
    b_rep_i = jnp.concatenate([jnp.broadcast_to(bbi[c:c + 1], (C, 2 * P)) for c in range(C)], axis=0)
    c_tile_r = jnp.concatenate([crr_ref[0]] * C, axis=0)
    c_tile_i = jnp.concatenate([cri_ref[0]] * C, axis=0)
    bcr, bci = _cmul(b_rep_r, b_rep_i, c_tile_r, c_tile_i)

    clre, clim, cldt = colp_ref[0, :, 0:1], colp_ref[0, :, 1:2], colp_ref[0, :, 2:3]
    cbr, cbi = _lambda_bar(clre, clim, cldt)
    sub = lax.broadcasted_iota(jnp.int32, (2 * P, LANES), 0)
    m_lane = lax.broadcasted_iota(jnp.int32, (2 * P, LANES), 1)

    pgr, pgi = _cpow_table(cbr, cbi, jnp.abs(m_lane - (L - 1)), nbits)
    live = ((sub < P) & (m_lane >= L - 1)) | ((sub >= P) & (m_lane <= L - 1))
    pgr = jnp.where(live, pgr, 0.0)
    pgi = jnp.where(live, pgi, 0.0)
    g = _dot_exact(bcr, pgr) - _dot_exact(bci, pgi)
    g_row = lax.broadcasted_iota(jnp.int32, (C * C, LANES), 0)
    g_lane = lax.broadcasted_iota(jnp.int32, (C * C, LANES), 1)
    log2c = C.bit_length() - 1
    skip = ((g_row >> log2c) == (g_row & (C - 1))) & (g_lane == L - 1)
    g_s[...] = g + jnp.where(skip, dcol_ref[0], 0.0)

    j_lane = m_lane & (L - 1)
    pyr, pyi = _cpow_table(cbr, cbi, jnp.where(sub < P, j_lane + 1, L - j_lane), nbits)
    ctr, cti = ctr_ref[0], cti_ref[0]
    low = m_lane < L
    for a in range(C // 2):
        cr = jnp.where(low, ctr[:, 2 * a:2 * a + 1], ctr[:, 2 * a + 1:2 * a + 2])
        ci = jnp.where(low, cti[:, 2 * a:2 * a + 1], cti[:, 2 * a + 1:2 * a + 2])
        yr, yi = _cmul(cr, ci, pyr, pyi)
        wy_s[0:2 * P, a * LANES:(a + 1) * LANES] = yr.astype(BF16)
        wy_s[2 * P:4 * P, a * LANES:(a + 1) * LANES] = (-yi).astype(BF16)

    return dr, di


def _fill_toeplitz(g_s, t_s, c_ins):
    L, C = SSM_CHUNK, SSM_GROUP
    t_low = lax.broadcasted_iota(jnp.int32, (L, LANES), 1) < L
    for c_in in c_ins:
        for a in range(C // 2):
            row = c_in * C + 2 * a
            ge = jnp.broadcast_to(g_s[row:row + 1, :], (L, LANES))
            go = jnp.broadcast_to(g_s[row + 1:row + 2, :], (L, LANES))
            te = pltpu.roll(ge, L + 1, 1, stride=1, stride_axis=0)
            to = pltpu.roll(go, 1, 1, stride=1, stride_axis=0)
            t_s[c_in * L:(c_in + 1) * L, a * LANES:(a + 1) * LANES] = jnp.where(t_low, te, to).astype(BF16)


def _ssm_kernel(*refs, seqs):
    P, L, C = SSM_STATE, SSM_CHUNK, SSM_GROUP
    ns = len(seqs)
    u_refs, refs = refs[:ns], refs[ns:]
    param_refs, refs = refs[:9], refs[9:]
    y_refs, refs = refs[:ns], refs[ns:]
    t_s, wz_s, wy_s, g_s, a_s, yt_s, zr_s, zi_s, wr_s, wi_s, fr_s, fi_s, rr_s, ri_s = refs
    blocks = [nseq * (slen // TOK_BLOCK) for nseq, slen in seqs]
    nblk = sum(blocks)
    ar, ai = _ssm_weights(*param_refs, wz_s, wy_s, g_s)
    _fill_toeplitz(g_s, t_s, range(0, C // 2))

    blk0 = 0
    for u_ref, nb in zip(u_refs, blocks):
        a_s[blk0 * C:(blk0 + nb) * C, :] = u_ref[...].astype(F32).reshape(nb * C, TOK_BLOCK)
        blk0 += nb

    src = lax.broadcasted_iota(jnp.int32, (2 * LANES, 2 * LANES), 0)
    dst = lax.broadcasted_iota(jnp.int32, (2 * LANES, 2 * LANES), 1)
    swap = ((src & (L - 1)) | ((src & L) << 1) | ((src & LANES) >> 1)) == dst
    perm = jnp.where(swap, 1.0, 0.0).astype(BF16)
    first, second = [], []
    for a in range(C // 2):
        x01 = jnp.concatenate([a_s[pl.ds(2 * a, nblk, stride=C), :],
                               a_s[pl.ds(2 * a + 1, nblk, stride=C), :]], axis=1).astype(BF16)
        halves = _dot(x01, perm).astype(BF16)
        first.append(halves[:, 0:LANES])
        second.append(halves[:, LANES:2 * LANES])
    u = jnp.concatenate([jnp.concatenate(first, axis=1), jnp.concatenate(second, axis=1)], axis=0)

    z = _dot(u, wz_s[...])
    zr_s[...] = z[:, 0:2 * P]
    zi_s[...] = z[:, 2 * P:4 * P]

    fwd = lax.broadcasted_iota(jnp.int32, (nblk, 2 * P), 1) < P
    z1r, z1i = z[0:nblk, 0:2 * P], z[0:nblk, 2 * P:4 * P]
    z2r, z2i = z[nblk:2 * nblk, 0:2 * P], z[nblk:2 * nblk, 2 * P:4 * P]
    wr_s[...] = jnp.where(fwd, ar * z1r - ai * z1i + z2r, ar * z2r - ai * z2i + z1r)
    wi_s[...] = jnp.where(fwd, ar * z1i + ai * z1r + z2i, ar * z2i + ai * z2r + z1i)
    a2r, a2i = _cmul(ar, ai, ar, ai)

    def scan(blk0, nseq, n):
        is_fwd = lax.broadcasted_iota(jnp.int32, (nseq, 2 * P), 1) < P

        def step(t, carry):
            sr, si = carry
            f_rows = pl.ds(blk0 + t, nseq, stride=n)
            b_rows = pl.ds(blk0 + n - 1 - t, nseq, stride=n)
            fr_s[f_rows, :] = sr
            fi_s[f_rows, :] = si
            rr_s[b_rows, :] = sr
            ri_s[b_rows, :] = si
            wr = jnp.where(is_fwd, wr_s[f_rows, :], wr_s[b_rows, :])
            wi = jnp.where(is_fwd, wi_s[f_rows, :], wi_s[b_rows, :])
            return a2r * sr - a2i * si + wr, a2r * si + a2i * sr + wi

        zero = jnp.zeros((nseq, 2 * P), F32)
        lax.fori_loop(0, n, step, (zero, zero))

    blk0 = 0
    for (nseq, slen), nb in zip(seqs, blocks):
        scan(blk0, nseq, slen // TOK_BLOCK)
        blk0 += nb

    fr, fi, rr, ri = fr_s[...], fi_s[...], rr_s[...], ri_s[...]
    z1r, z1i, z2r, z2i = zr_s[0:nblk, :], zi_s[0:nblk, :], zr_s[nblk:2 * nblk, :], zi_s[nblk:2 * nblk, :]
    s1r = jnp.where(fwd, fr, ar * rr - ai * ri + z2r)
    s1i = jnp.where(fwd, fi, ar * ri + ai * rr + z2i)
    s2r = jnp.where(fwd, ar * fr - ai * fi + z1r, rr)
    s2i = jnp.where(fwd, ar * fi + ai * fr + z1i, ri)
    s_in = jnp.concatenate([jnp.concatenate([s1r, s2r], axis=0),
                            jnp.concatenate([s1i, s2i], axis=0)], axis=1).astype(BF16)
    _fill_toeplitz(g_s, t_s, range(C // 2, C))
    half = CHUNK_W // 2
    y = (_dot(u[:, :half], t_s[0:half, :]) + _dot(s_in, wy_s[...])
         + _dot(u[:, half:], t_s[half:CHUNK_W, :]))

    for a in range(C // 2):
        y01 = jnp.concatenate([y[0:nblk, a * LANES:(a + 1) * LANES],
                               y[nblk:2 * nblk, a * LANES:(a + 1) * LANES]], axis=1).astype(BF16)
        chans = _dot(y01, perm)
        yt_s[pl.ds(2 * a, nblk, stride=C), :] = chans[:, 0:LANES]
        yt_s[pl.ds(2 * a + 1, nblk, stride=C), :] = chans[:, LANES:2 * LANES]
    blk0 = 0
    for y_ref, nb in zip(y_refs, blocks):
        y_ref[...] = yt_s[blk0 * C:(blk0 + nb) * C, :].reshape(nb, C, TOK_BLOCK).astype(y_ref.dtype)
        blk0 += nb


def _ssm(uts, seqs, lam_re, lam_im, log_dt, b_re, b_im, c_re, c_im, d_skip):
    width = uts[0].shape[1]
    _, groups, P = lam_re.shape
    C = SSM_GROUP
    assert P == SSM_STATE and b_re.shape[-1] == C and groups * C == width
    assert all(s % TOK_BLOCK == 0 for _, s in seqs)
    assert all(ut.shape == (b * s // TOK_BLOCK, width, TOK_BLOCK) for ut, (b, s) in zip(uts, seqs))
    nblk = sum(ut.shape[0] for ut in uts)

    def lanes_fb(a):
        return jnp.transpose(a, (1, 0, 2)).reshape(groups, 2 * P)

    ldt = jnp.broadcast_to(log_dt[:, :, None], (2, groups, P))
    rowp = jnp.stack([lanes_fb(lam_re), lanes_fb(lam_im), lanes_fb(ldt)], axis=1)
    colp = jnp.transpose(rowp, (0, 2, 1))
    bt = lambda b: jnp.transpose(b, (1, 3, 0, 2)).reshape(groups, C, 2 * P)
    cr = lambda c: jnp.transpose(c, (1, 2, 0, 3)).reshape(groups, C, 2 * P)
    ct = lambda c: jnp.transpose(c, (1, 0, 3, 2)).reshape(groups, 2 * P, C)
    dcol = jnp.tile(d_skip.reshape(groups, 1, C), (1, C, 1)).reshape(groups, C * C, 1)

    g3 = lambda s1, s2: pl.BlockSpec((1, s1, s2), lambda g: (g, 0, 0))
    acts = [pl.BlockSpec((ut.shape[0], C, TOK_BLOCK), lambda g: (0, g, 0)) for ut in uts]
    return pl.pallas_call(
        functools.partial(_ssm_kernel, seqs=tuple(seqs)),
        out_shape=[jax.ShapeDtypeStruct(ut.shape, BF16) for ut in uts],
        grid=(groups,),
        in_specs=acts + [g3(3, 2 * P), g3(2 * P, 3), g3(C, 2 * P), g3(C, 2 * P), g3(C, 2 * P), g3(C, 2 * P),
                         g3(2 * P, C), g3(2 * P, C), g3(C * C, 1)],
        out_specs=acts,
        scratch_shapes=[
            pltpu.VMEM((CHUNK_W, CHUNK_W), BF16),
            pltpu.VMEM((CHUNK_W, 4 * P), BF16),
            pltpu.VMEM((4 * P, CHUNK_W), BF16),
            pltpu.VMEM((C * C, LANES), F32),
            pltpu.VMEM((nblk * C, TOK_BLOCK), F32),
            pltpu.VMEM((nblk * C, TOK_BLOCK), F32),
        ] + [pltpu.VMEM((2 * nblk, 2 * P), F32)] * 2 + [pltpu.VMEM((nblk, 2 * P), F32)] * 6,
        compiler_params=_cparams(("parallel",), 40 << 20),
        name="ssm",
    )(*uts, rowp, colp, bt(b_re), bt(b_im), cr(c_re), cr(c_im), ct(c_re), ct(c_im), dcol)


def _merge_kernel(x_ref, a_ref, yt_ref, gate_ref, wglu_ref, bglu_ref, wao_ref, wso_ref, wout_ref, o_ref):
    d = x_ref.shape[1]
    y = jnp.concatenate([yt_ref[j].astype(F32).T for j in range(yt_ref.shape[0])], axis=0)
    z = jax.nn.gelu(y)
    z = z * jax.nn.sigmoid(_dot(z.astype(BF16), wglu_ref[...]) + bglu_ref[...])
    m = _dot(z.astype(BF16), wso_ref[...])
    a = _dot(a_ref[...], wao_ref[...])
    g_attn, g_ssm = gate_ref[:, 0:d].astype(F32), gate_ref[:, d:2 * d].astype(F32)
    merged = jax.nn.sigmoid(g_attn) * a + jax.nn.sigmoid(g_ssm) * m
    o_ref[...] = x_ref[...] + _dot(merged.astype(BF16), wout_ref[...])


def _merge(x, attn, yt, gates, w_glu, b_glu, w_ao, w_so, w_out):
    t, d = x.shape
    aw, sw = attn.shape[1], yt.shape[1]
    tm = MERGE_ROWS
    weights = (sw * sw + aw * d + sw * d + d * d) * 2
    vmem = weights + 2 * tm * (2 * d * 4 + (aw + sw + 2 * d) * 2) + 6 * tm * d * 4 + (4 << 20)
    return pl.pallas_call(
        _merge_kernel,
        out_shape=jax.ShapeDtypeStruct((t, d), F32),
        grid=(t // tm,),
        in_specs=[
            pl.BlockSpec((tm, d), lambda i: (i, 0)),
            pl.BlockSpec((tm, aw), lambda i: (i, 0)),
            pl.BlockSpec((tm // TOK_BLOCK, sw, TOK_BLOCK), lambda i: (i, 0, 0)),
            pl.BlockSpec((tm, 2 * d), lambda i: (i, 0)),
            _const_spec((sw, sw)), _const_spec((1, sw)), _const_spec((aw, d)), _const_spec((sw, d)),
            _const_spec((d, d)),
        ],
        out_specs=pl.BlockSpec((tm, d), lambda i: (i, 0)),
        compiler_params=_cparams(("parallel",), vmem),
        name="mixer_out",
    )(x, attn, yt, gates, w_glu, b_glu.reshape(1, sw), w_ao, w_so, w_out)


def _layer(xs, ffn1_norm, ffn1_w_gate_up, ffn1_w_down, mix_norm, w_in, q_norm, k_norm, attn_sink,
           ssm_lambda_re, ssm_lambda_im, ssm_log_dt, ssm_b_re, ssm_b_im, ssm_c_re, ssm_c_im, ssm_d,
           w_glu, b_glu, w_attn_out, w_ssm_out, w_out, ffn2_norm, ffn2_w_gate_up, ffn2_w_down):
    d = xs[0].shape[-1]
    seqs = [(x.shape[0], x.shape[1]) for x in xs]
    aw = N_Q_HEADS * HEAD_DIM
    kw = N_KV_HEADS * HEAD_DIM
    sw = ssm_d.shape[0]

    o_u, o_g = aw + 2 * kw, aw + 2 * kw + sw
    w_qkv = w_in[:, :o_u].astype(BF16)
    w_u_t = w_in[:, o_u:o_g].T.astype(BF16)
    w_gates = w_in[:, o_g:].astype(BF16)
    w1_gu, w1_d = _ffn_weights(ffn1_w_gate_up, ffn1_w_down)
    w2_gu, w2_d = _ffn_weights(ffn2_w_gate_up, ffn2_w_down)
    w_glu_b, w_ao, w_so, w_o = (w.astype(BF16) for w in (w_glu, w_attn_out, w_ssm_out, w_out))
    rope_tab = _rope_table(max(s for _, s in seqs))

    x1s, attns, gates, uts = [], [], [], []
    for x, (b, s) in zip(xs, seqs):
        x1 = _ffn(x.reshape(b * s, d), ffn1_norm, w1_gu, w1_d)
        qkv, ut, gate = _inproj(x1, mix_norm, w_qkv, w_u_t, w_gates, s, rope_tab, q_norm, k_norm)
        uts.append(ut)
        gates.append(gate)
        attns.append(_attention(qkv, b, s, attn_sink))
        x1s.append(x1)

    yts = _ssm(uts, seqs, ssm_lambda_re, ssm_lambda_im, ssm_log_dt, ssm_b_re, ssm_b_im, ssm_c_re, ssm_c_im, ssm_d)

    outs = []
    for x_in, x1, attn, yt, gate in zip(xs, x1s, attns, yts, gates):
        x2 = _merge(x1, attn, yt, gate, w_glu_b, b_glu, w_ao, w_so, w_o)
        outs.append(_ffn(x2, ffn2_norm, w2_gu, w2_d).reshape(x_in.shape))
    return tuple(outs)


def kernel(x_prompt, x_sample, ffn1_norm, ffn1_w_gate_up, ffn1_w_down, mix_norm, w_in, q_norm, k_norm, attn_sink, ssm_lambda_re, ssm_lambda_im, ssm_log_dt, ssm_b_re, ssm_b_im, ssm_c_re, ssm_c_im, ssm_d, w_glu, b_glu, w_attn_out, w_ssm_out, w_out, ffn2_norm, ffn2_w_gate_up, ffn2_w_down):
    xs = (x_prompt, x_sample)
    depth = ffn1_norm.shape[0]
    params = (ffn1_norm, ffn1_w_gate_up, ffn1_w_down, mix_norm, w_in, q_norm, k_norm, attn_sink,
              ssm_lambda_re, ssm_lambda_im, ssm_log_dt, ssm_b_re, ssm_b_im, ssm_c_re, ssm_c_im, ssm_d,
              w_glu, b_glu, w_attn_out, w_ssm_out, w_out, ffn2_norm, ffn2_w_gate_up, ffn2_w_down)
    for l in range(depth):
        xs = _layer(xs, *(p[l] for p in params))
    return xs
```

```python
import functools
import math

import jax
import jax.numpy as jnp
import numpy as np
from jax import lax
from jax.experimental import pallas as pl
from jax.experimental.pallas import tpu as pltpu

F32 = jnp.float32
BF16 = jnp.bfloat16

HEAD_DIM = 128
N_Q_HEADS = 8
N_KV_HEADS = 2
Q_PER_KV = N_Q_HEADS // N_KV_HEADS
WINDOW = 128
ATTN_BLOCK = 128
ROPE_THETA = 10000.0
SSM_GROUP = 16
SSM_STATE = 64
EPS = 1e-6
LOG2E = math.log2(math.e)

LANES = 128
VMEM_LIMIT_CAP = 56 * 1024 * 1024

SSM_CHUNK = 64
CHUNK_W = SSM_CHUNK * SSM_GROUP
TOK_BLOCK = 128
FFN_ROWS = 512
FFN_COLS = 512
INPROJ_ROWS = 256
MERGE_ROWS = 512
ATTN_QBLOCKS = 8

assert 2 * SSM_CHUNK == TOK_BLOCK == LANES and 2 * SSM_STATE == LANES


def _cparams(semantics, vmem_bytes):
    return pltpu.CompilerParams(dimension_semantics=semantics,
                                vmem_limit_bytes=min(int(vmem_bytes), VMEM_LIMIT_CAP))


def _rms(x, gain):
    return x * lax.rsqrt(jnp.mean(x * x, axis=-1, keepdims=True) + EPS) * gain


def _dot(a, b):
    return jnp.dot(a, b, preferred_element_type=F32)


def _dot_exact(a, b):
    return jnp.dot(a, b, preferred_element_type=F32, precision=lax.Precision.HIGHEST)


def _ffn_kernel(x_ref, gain_ref, wg_ref, wu_ref, wd_ref, o_ref, h_ref):
    @pl.when(pl.program_id(1) == 0)
    def _():
        x = x_ref[...]
        h_ref[...] = _rms(x, gain_ref[...]).astype(BF16)
        o_ref[...] = x

    h = h_ref[...]
    g = _dot(h, wg_ref[...])
    u = _dot(h, wu_ref[...])
    a = (g * jax.nn.sigmoid(g)) * u
    o_ref[...] += _dot(a.astype(BF16), wd_ref[...])


def _ffn_weights(w_gate_up, w_down):
    return w_gate_up.astype(BF16), (0.5 * w_down).astype(BF16)


def _ffn(x, gain, w_gate_up, w_down_half):
    t, d = x.shape
    d_ff = w_down_half.shape[0]
    tm, tf = FFN_ROWS, FFN_COLS
    nf = d_ff // tf
    vmem = (4 * tm * d * 4) + (tm * d * 2) + 2 * 3 * (d * tf * 2) + 4 * (tm * tf * 4) + (4 << 20)
    return pl.pallas_call(
        _ffn_kernel,
        out_shape=jax.ShapeDtypeStruct((t, d), F32),
        grid=(t // tm, nf),
        in_specs=[
            pl.BlockSpec((tm, d), lambda i, f: (i, 0)),
            pl.BlockSpec((1, d), lambda i, f: (0, 0)),
            pl.BlockSpec((d, tf), lambda i, f: (0, f)),
            pl.BlockSpec((d, tf), lambda i, f: (0, f + nf)),
            pl.BlockSpec((tf, d), lambda i, f: (f, 0)),
        ],
        out_specs=pl.BlockSpec((tm, d), lambda i, f: (i, 0)),
        scratch_shapes=[pltpu.VMEM((tm, d), BF16)],
        compiler_params=_cparams(("parallel", "arbitrary"), vmem),
        name="ffn",
    )(x, gain.reshape(1, d), w_gate_up, w_gate_up, w_down_half)


def _rope(x, cos2, sin2):
    return x * cos2 + pltpu.roll(x, HEAD_DIM // 2, 1) * sin2


def _inproj_kernel(x_ref, gain_ref, wqkv_ref, wut_ref, wg_ref, rope_ref, qg_ref, kg_ref,
                   qkv_ref, ut_ref, gates_ref):
    h = _rms(x_ref[...], gain_ref[...]).astype(BF16)

    r = _dot(h, wqkv_ref[...])
    cos2, sin2 = rope_ref[:, 0:HEAD_DIM], rope_ref[:, HEAD_DIM:2 * HEAD_DIM]
    qg = qg_ref[...] * (LOG2E * HEAD_DIM ** -0.5)
    kg = kg_ref[...]
    n_rot = N_Q_HEADS + N_KV_HEADS
    for head in range(n_rot):
        cols = slice(head * HEAD_DIM, (head + 1) * HEAD_DIM)
        gain = qg if head < N_Q_HEADS else kg
        qkv_ref[:, cols] = _rope(_rms(r[:, cols], gain), cos2, sin2).astype(BF16)
    vcols = slice(n_rot * HEAD_DIM, (n_rot + N_KV_HEADS) * HEAD_DIM)
    qkv_ref[:, vcols] = r[:, vcols].astype(BF16)

    rt = lax.dot_general(wut_ref[...], h, (((1,), (1,)), ((), ())), preferred_element_type=F32)
    for j in range(ut_ref.shape[0]):
        ut_ref[j] = rt[:, j * TOK_BLOCK:(j + 1) * TOK_BLOCK].astype(BF16)

    gates_ref[...] = _dot(h, wg_ref[...]).astype(BF16)


def _const_spec(shape):
    return pl.BlockSpec(shape, lambda *_: (0,) * len(shape), pipeline_mode=pl.Buffered(1))


def _inproj(x, gain, w_qkv, w_u_t, w_gates, seq, rope_tab, q_gain, k_gain):
    t, d = x.shape
    nqkv, nu, ng = w_qkv.shape[1], w_u_t.shape[0], w_gates.shape[1]
    tm = math.gcd(INPROJ_ROWS, seq)
    tiles_per_seq = seq // tm
    n_all = nqkv + nu + ng
    vmem = d * n_all * 2 + 2 * tm * d * 4 + 2 * tm * n_all * 2 + tm * n_all * 4 + tm * d * 6 + (6 << 20)
    row = lambda n: pl.BlockSpec((tm, n), lambda i: (i, 0))
    table = pl.BlockSpec((tm, 2 * HEAD_DIM), lambda i: (i % tiles_per_seq, 0))
    return pl.pallas_call(
        _inproj_kernel,
        out_shape=(jax.ShapeDtypeStruct((t, nqkv), BF16),
                   jax.ShapeDtypeStruct((t // TOK_BLOCK, nu, TOK_BLOCK), BF16),
                   jax.ShapeDtypeStruct((t, ng), BF16)),
        grid=(t // tm,),
        in_specs=[row(d), _const_spec((1, d)), _const_spec((d, nqkv)), _const_spec((nu, d)), _const_spec((d, ng)),
                  table, _const_spec((1, HEAD_DIM)), _const_spec((1, HEAD_DIM))],
        out_specs=(row(nqkv), pl.BlockSpec((tm // TOK_BLOCK, nu, TOK_BLOCK), lambda i: (i, 0, 0)), row(ng)),
        compiler_params=_cparams(("parallel",), vmem),
        name="inproj",
    )(x, gain.reshape(1, d), w_qkv, w_u_t, w_gates, rope_tab,
      q_gain.reshape(1, HEAD_DIM), k_gain.reshape(1, HEAD_DIM))


def _rope_table_kernel(freq_ref, tab_ref):
    rows = tab_ref.shape[0]
    pos = (lax.broadcasted_iota(jnp.int32, (rows, HEAD_DIM), 0) + pl.program_id(0) * rows).astype(F32)
    ang = pos * freq_ref[...]
    lane = lax.broadcasted_iota(jnp.int32, (rows, HEAD_DIM), 1)
    tab_ref[:, 0:HEAD_DIM] = jnp.cos(ang)
    tab_ref[:, HEAD_DIM:2 * HEAD_DIM] = jnp.where(lane < HEAD_DIM // 2, -1.0, 1.0) * jnp.sin(ang)


def _rope_table(seq):
    inv_freq = ROPE_THETA ** (-jnp.arange(0, HEAD_DIM, 2, dtype=F32) / HEAD_DIM)
    freq2 = jnp.concatenate([inv_freq, inv_freq]).reshape(1, HEAD_DIM)
    rows = math.gcd(seq, 512)
    return pl.pallas_call(
        _rope_table_kernel,
        out_shape=jax.ShapeDtypeStruct((seq, 2 * HEAD_DIM), F32),
        grid=(seq // rows,),
        in_specs=[pl.BlockSpec((1, HEAD_DIM), lambda i: (0, 0))],
        out_specs=pl.BlockSpec((rows, 2 * HEAD_DIM), lambda i: (i, 0)),
        compiler_params=_cparams(("parallel",), 16 << 20),
        name="rope_table",
    )(freq2)


def _attn_bias(nq):
    blk = ATTN_BLOCK
    kj = np.arange(3 * blk)[:, None]
    qi = np.arange(nq)[None, :] % blk
    ok = np.abs(kj - blk - qi) <= WINDOW
    variants = []
    for v in range(4):
        valid = ok & ((kj >= blk) | (v & 1 == 0)) & ((kj < 2 * blk) | (v & 2 == 0))
        variants.append(np.where(valid, 0.0, -np.inf))
    return jnp.asarray(np.stack(variants), F32)


def _attn_kernel(sink_ref, bias_ref, q_ref, kvp_ref, kvc_ref, kvn_ref, o_ref):
    n = pl.program_id(1)
    blk = ATTN_BLOCK
    nqb = q_ref.shape[0] // blk
    kw = N_KV_HEADS * HEAD_DIM
    no_prev = jnp.where(n == 0, 1, 0)
    no_next = jnp.where(n == pl.num_programs(1) - 1, 2, 0)

    for h in range(N_KV_HEADS):
        kc = slice(h * HEAD_DIM, (h + 1) * HEAD_DIM)
        vc = slice(kw + h * HEAD_DIM, kw + (h + 1) * HEAD_DIM)
        blocks = [kvp_ref] + [kvc_ref.at[qb * blk:(qb + 1) * blk] for qb in range(nqb)] + [kvn_ref]
        k_parts = [ref[:, kc] for ref in blocks]
        v_parts = [ref[:, vc] for ref in blocks]
        heads = range(h * Q_PER_KV, (h + 1) * Q_PER_KV)
        sink = jnp.concatenate([jnp.full((1, blk), sink_ref[hd] * LOG2E, F32) for hd in heads], axis=1)
        for qb in range(nqb):
            rows = slice(qb * blk, (qb + 1) * blk)
            variant = (no_prev if qb == 0 else 0) + (no_next if qb == nqb - 1 else 0)
            k_band = jnp.concatenate(k_parts[qb:qb + 3], axis=0)
            v_band = jnp.concatenate(v_parts[qb:qb + 3], axis=0)
            q_all = jnp.concatenate([q_ref[rows, hd * HEAD_DIM:(hd + 1) * HEAD_DIM] for hd in heads], axis=0)
            s = lax.dot_general(k_band, q_all, (((1,), (1,)), ((), ())), preferred_element_type=F32)
            s = s + bias_ref[variant]
            m = jnp.maximum(jnp.max(s, axis=0, keepdims=True), sink)
            p = jnp.exp2(s - m)
            denom = jnp.sum(p, axis=0, keepdims=True) + jnp.exp2(sink - m)
            ot = lax.dot_general(v_band, p.astype(BF16), (((0,), (0,)), ((), ())), preferred_element_type=F32)
            ot = ot / denom
            for g, hd in enumerate(heads):
                o_ref[rows, hd * HEAD_DIM:(hd + 1) * HEAD_DIM] = ot[:, g * blk:(g + 1) * blk].T.astype(BF16)


def _attention(qkv, batch, seq, sink):
    blk = ATTN_BLOCK
    nqb = math.gcd(ATTN_QBLOCKS, seq // blk)
    step = nqb * blk
    nb = seq // step
    qw = N_Q_HEADS * HEAD_DIM
    kvw = 2 * N_KV_HEADS * HEAD_DIM
    assert qw % kvw == 0 and seq % blk == 0
    kv_col = qw // kvw
    nq = Q_PER_KV * blk
    grid_spec = pltpu.PrefetchScalarGridSpec(
        num_scalar_prefetch=1,
        grid=(batch, nb),
        in_specs=[
            _const_spec((4, 3 * blk, nq)),
            pl.BlockSpec((step, qw), lambda b, n, s: (b * nb + n, 0)),
            pl.BlockSpec((blk, kvw), lambda b, n, s: (nqb * (b * nb) + jnp.maximum(nqb * n - 1, 0), kv_col)),
            pl.BlockSpec((step, kvw), lambda b, n, s: (b * nb + n, kv_col)),
            pl.BlockSpec((blk, kvw),
                         lambda b, n, s: (nqb * (b * nb) + jnp.minimum(nqb * n + nqb, nqb * nb - 1), kv_col)),
        ],
        out_specs=pl.BlockSpec((step, qw), lambda b, n, s: (b * nb + n, 0)),
    )
    return pl.pallas_call(
        _attn_kernel,
        out_shape=jax.ShapeDtypeStruct((batch * seq, qw), BF16),
        grid_spec=grid_spec,
        compiler_params=_cparams(("parallel", "parallel"), 40 << 20),
        name="attention",
    )(sink, _attn_bias(nq), qkv, qkv, qkv, qkv)


def _cmul(ar, ai, br, bi):
    return ar * br - ai * bi, ar * bi + ai * br


def _cpow_table(br, bi, expo, nbits):
    pr = jnp.where((expo & 1) == 1, br, 1.0)
    pi = jnp.where((expo & 1) == 1, bi, 0.0)
    for k in range(1, nbits):
        br, bi = _cmul(br, bi, br, bi)
        bit = ((expo >> k) & 1) == 1
        nr, ni = _cmul(pr, pi, br, bi)
        pr = jnp.where(bit, nr, pr)
        pi = jnp.where(bit, ni, pi)
    return pr, pi


def _lambda_bar(lre, lim, log_dt):
    dt = jnp.exp(log_dt)
    mag = jnp.exp(lre * dt)
    return mag * jnp.cos(lim * dt), mag * jnp.sin(lim * dt)


def _ssm_weights(rowp_ref, colp_ref, btr_ref, bti_ref, crr_ref, cri_ref, ctr_ref, cti_ref, dcol_ref,
                 wz_s, wy_s, g_s):
    L, P, C = SSM_CHUNK, SSM_STATE, SSM_GROUP
    nbits = L.bit_length()

    lre, lim, ldt = rowp_ref[0, 0:1, :], rowp_ref[0, 1:2, :], rowp_ref[0, 2:3, :]
    lbr, lbi = _lambda_bar(lre, lim, ldt)
    den = lre * lre + lim * lim
    cfr = ((lbr - 1.0) * lre + lbi * lim) / den
    cfi = (lbi * lre - (lbr - 1.0) * lim) / den
    bbr, bbi = _cmul(cfr, cfi, btr_ref[0], bti_ref[0])

    dr, di = lbr, lbi
    for _ in range(L.bit_length() - 1):
        dr, di = _cmul(dr, di, dr, di)

    n_sub = lax.broadcasted_iota(jnp.int32, (L, 2 * P), 0)
    lane = lax.broadcasted_iota(jnp.int32, (L, 2 * P), 1)
    pzr, pzi = _cpow_table(lbr, lbi, jnp.where(lane < P, L - 1 - n_sub, n_sub), nbits)
    for c in range(C):
        zr, zi = _cmul(pzr, pzi, bbr[c:c + 1], bbi[c:c + 1])
        wz_s[c * L:(c + 1) * L, 0:2 * P] = zr.astype(BF16)
        wz_s[c * L:(c + 1) * L, 2 * P:4 * P] = zi.astype(BF16)

    b_rep_r = jnp.concatenate([jnp.broadcast_to(bbr[c:c + 1], (C, 2 * P)) for c in range(C)], axis=0)
    b_rep_i = jnp.concatenate([jnp.broadcast_to(bbi[c:c + 1], (C, 2 * P)) for c in range(C)], axis=0)
    c_tile_r = jnp.concatenate([crr_ref[0]] * C, axis=0)
    c_tile_i = jnp.concatenate([cri_ref[0]] * C, axis=0)
    bcr, bci = _cmul(b_rep_r, b_rep_i, c_tile_r, c_tile_i)

    clre, clim, cldt = colp_ref[0, :, 0:1], colp_ref[0, :, 1:2], colp_ref[0, :, 2:3]
    cbr, cbi = _lambda_bar(clre, clim, cldt)
    sub = lax.broadcasted_iota(jnp.int32, (2 * P, LANES), 0)
    m_lane = lax.broadcasted_iota(jnp.int32, (2 * P, LANES), 1)

    pgr, pgi = _cpow_table(cbr, cbi, jnp.abs(m_lane - (L - 1)), nbits)
    live = ((sub < P) & (m_lane >= L - 1)) | ((sub >= P) & (m_lane <= L - 1))
    pgr = jnp.where(live, pgr, 0.0)
    pgi = jnp.where(live, pgi, 0.0)
    g = _dot_exact(bcr, pgr) - _dot_exact(bci, pgi)
    g_row = lax.broadcasted_iota(jnp.int32, (C * C, LANES), 0)
    g_lane = lax.broadcasted_iota(jnp.int32, (C * C, LANES), 1)
    log2c = C.bit_length() - 1
    skip = ((g_row >> log2c) == (g_row & (C - 1))) & (g_lane == L - 1)
    g_s[...] = g + jnp.where(skip, dcol_ref[0], 0.0)

    j_lane = m_lane & (L - 1)
    pyr, pyi = _cpow_table(cbr, cbi, jnp.where(sub < P, j_lane + 1, L - j_lane), nbits)
    ctr, cti = ctr_ref[0], cti_ref[0]
    low = m_lane < L
    for a in range(C // 2):
        cr = jnp.where(low, ctr[:, 2 * a:2 * a + 1], ctr[:, 2 * a + 1:2 * a + 2])
        ci = jnp.where(low, cti[:, 2 * a:2 * a + 1], cti[:, 2 * a + 1:2 * a + 2])
        yr, yi = _cmul(cr, ci, pyr, pyi)
        wy_s[0:2 * P, a * LANES:(a + 1) * LANES] = yr.astype(BF16)
        wy_s[2 * P:4 * P, a * LANES:(a + 1) * LANES] = (-yi).astype(BF16)

    return dr, di


def _fill_toeplitz(g_s, t_s, c_ins):
    L, C = SSM_CHUNK, SSM_GROUP
    t_low = lax.broadcasted_iota(jnp.int32, (L, LANES), 1) < L
    for c_in in c_ins:
        for a in range(C // 2):
            row = c_in * C + 2 * a
            ge = jnp.broadcast_to(g_s[row:row + 1, :], (L, LANES))
            go = jnp.broadcast_to(g_s[row + 1:row + 2, :], (L, LANES))
            te = pltpu.roll(ge, L + 1, 1, stride=1, stride_axis=0)
            to = pltpu.roll(go, 1, 1, stride=1, stride_axis=0)
            t_s[c_in * L:(c_in + 1) * L, a * LANES:(a + 1) * LANES] = jnp.where(t_low, te, to).astype(BF16)


def _ssm_kernel(*refs, seqs):
    P, L, C = SSM_STATE, SSM_CHUNK, SSM_GROUP
    ns = len(seqs)
    u_refs, refs = refs[:ns], refs[ns:]
    param_refs, refs = refs[:9], refs[9:]
    y_refs, refs = refs[:ns], refs[ns:]
    t_s, wz_s, wy_s, g_s, a_s, yt_s, zr_s, zi_s, wr_s, wi_s, fr_s, fi_s, rr_s, ri_s = refs
    blocks = [nseq * (slen // TOK_BLOCK) for nseq, slen in seqs]
    nblk = sum(blocks)
    ar, ai = _ssm_weights(*param_refs, wz_s, wy_s, g_s)
    _fill_toeplitz(g_s, t_s, range(0, C // 2))

    blk0 = 0
    for u_ref, nb in zip(u_refs, blocks):
        a_s[blk0 * C:(blk0 + nb) * C, :] = u_ref[...].astype(F32).reshape(nb * C, TOK_BLOCK)
        blk0 += nb

    src = lax.broadcasted_iota(jnp.int32, (2 * LANES, 2 * LANES), 0)
    dst = lax.broadcasted_iota(jnp.int32, (2 * LANES, 2 * LANES), 1)
    swap = ((src & (L - 1)) | ((src & L) << 1) | ((src & LANES) >> 1)) == dst
    perm = jnp.where(swap, 1.0, 0.0).astype(BF16)
    first, second = [], []
    for a in range(C // 2):
        x01 = jnp.concatenate([a_s[pl.ds(2 * a, nblk, stride=C), :],
                               a_s[pl.ds(2 * a + 1, nblk, stride=C), :]], axis=1).astype(BF16)
        halves = _dot(x01, perm).astype(BF16)
        first.append(halves[:, 0:LANES])
        second.append(halves[:, LANES:2 * LANES])
    u = jnp.concatenate([jnp.concatenate(first, axis=1), jnp.concatenate(second, axis=1)], axis=0)

    z = _dot(u, wz_s[...])
    zr_s[...] = z[:, 0:2 * P]
    zi_s[...] = z[:, 2 * P:4 * P]

    fwd = lax.broadcasted_iota(jnp.int32, (nblk, 2 * P), 1) < P
    z1r, z1i = z[0:nblk, 0:2 * P], z[0:nblk, 2 * P:4 * P]
    z2r, z2i = z[nblk:2 * nblk, 0:2 * P], z[nblk:2 * nblk, 2 * P:4 * P]
    wr_s[...] = jnp.where(fwd, ar * z1r - ai * z1i + z2r, ar * z2r - ai * z2i + z1r)
    wi_s[...] = jnp.where(fwd, ar * z1i + ai * z1r + z2i, ar * z2i + ai * z2r + z1i)
    a2r, a2i = _cmul(ar, ai, ar, ai)

    def scan(blk0, nseq, n):
        is_fwd = lax.broadcasted_iota(jnp.int32, (nseq, 2 * P), 1) < P

        def step(t, carry):
            sr, si = carry
            f_rows = pl.ds(blk0 + t, nseq, stride=n)
            b_rows = pl.ds(blk0 + n - 1 - t, nseq, stride=n)
            fr_s[f_rows, :] = sr
            fi_s[f_rows, :] = si
            rr_s[b_rows, :] = sr
            ri_s[b_rows, :] = si
            wr = jnp.where(is_fwd, wr_s[f_rows, :], wr_s[b_rows, :])
            wi = jnp.where(is_fwd, wi_s[f_rows, :], wi_s[b_rows, :])
            return a2r * sr - a2i * si + wr, a2r * si + a2i * sr + wi

        zero = jnp.zeros((nseq, 2 * P), F32)
        lax.fori_loop(0, n, step, (zero, zero))

    blk0 = 0
    for (nseq, slen), nb in zip(seqs, blocks):
        scan(blk0, nseq, slen // TOK_BLOCK)
        blk0 += nb

    fr, fi, rr, ri = fr_s[...], fi_s[...], rr_s[...], ri_s[...]
    z1r, z1i, z2r, z2i = zr_s[0:nblk, :], zi_s[0:nblk, :], zr_s[nblk:2 * nblk, :], zi_s[nblk:2 * nblk, :]
    s1r = jnp.where(fwd, fr, ar * rr - ai * ri + z2r)
    s1i = jnp.where(fwd, fi, ar * ri + ai * rr + z2i)
    s2r = jnp.where(fwd, ar * fr - ai * fi + z1r, rr)
    s2i = jnp.where(fwd, ar * fi + ai * fr + z1i, ri)
    s_in = jnp.concatenate([jnp.concatenate([s1r, s2r], axis=0),
                            jnp.concatenate([s1i, s2i], axis=0)], axis=1).astype(BF16)
    _fill_toeplitz(g_s, t_s, range(C // 2, C))
    half = CHUNK_W // 2
    y = (_dot(u[:, :half], t_s[0:half, :]) + _dot(s_in, wy_s[...])
         + _dot(u[:, half:], t_s[half:CHUNK_W, :]))

    for a in range(C // 2):
        y01 = jnp.concatenate([y[0:nblk, a * LANES:(a + 1) * LANES],
                               y[nblk:2 * nblk, a * LANES:(a + 1) * LANES]], axis=1).astype(BF16)
        chans = _dot(y01, perm)
        yt_s[pl.ds(2 * a, nblk, stride=C), :] = chans[:, 0:LANES]
        yt_s[pl.ds(2 * a + 1, nblk, stride=C), :] = chans[:, LANES:2 * LANES]
    blk0 = 0
    for y_ref, nb in zip(y_refs, blocks):
        y_ref[...] = yt_s[blk0 * C:(blk0 + nb) * C, :].reshape(nb, C, TOK_BLOCK).astype(y_ref.dtype)
        blk0 += nb


def _ssm(uts, seqs, lam_re, lam_im, log_dt, b_re, b_im, c_re, c_im, d_skip):
    width = uts[0].shape[1]
    _, groups, P = lam_re.shape
    C = SSM_GROUP
    assert P == SSM_STATE and b_re.shape[-1] == C and groups * C == width
    assert all(s % TOK_BLOCK == 0 for _, s in seqs)
    assert all(ut.shape == (b * s // TOK_BLOCK, width, TOK_BLOCK) for ut, (b, s) in zip(uts, seqs))
    nblk = sum(ut.shape[0] for ut in uts)

    def lanes_fb(a):
        return jnp.transpose(a, (1, 0, 2)).reshape(groups, 2 * P)

    ldt = jnp.broadcast_to(log_dt[:, :, None], (2, groups, P))
    rowp = jnp.stack([lanes_fb(lam_re), lanes_fb(lam_im), lanes_fb(ldt)], axis=1)
    colp = jnp.transpose(rowp, (0, 2, 1))
    bt = lambda b: jnp.transpose(b, (1, 3, 0, 2)).reshape(groups, C, 2 * P)
    cr = lambda c: jnp.transpose(c, (1, 2, 0, 3)).reshape(groups, C, 2 * P)
    ct = lambda c: jnp.transpose(c, (1, 0, 3, 2)).reshape(groups, 2 * P, C)
    dcol = jnp.tile(d_skip.reshape(groups, 1, C), (1, C, 1)).reshape(groups, C * C, 1)

    g3 = lambda s1, s2: pl.BlockSpec((1, s1, s2), lambda g: (g, 0, 0))
    acts = [pl.BlockSpec((ut.shape[0], C, TOK_BLOCK), lambda g: (0, g, 0)) for ut in uts]
    return pl.pallas_call(
        functools.partial(_ssm_kernel, seqs=tuple(seqs)),
        out_shape=[jax.ShapeDtypeStruct(ut.shape, BF16) for ut in uts],
        grid=(groups,),
        in_specs=acts + [g3(3, 2 * P), g3(2 * P, 3), g3(C, 2 * P), g3(C, 2 * P), g3(C, 2 * P), g3(C, 2 * P),
                         g3(2 * P, C), g3(2 * P, C), g3(C * C, 1)],
        out_specs=acts,
        scratch_shapes=[
            pltpu.VMEM((CHUNK_W, CHUNK_W), BF16),
            pltpu.VMEM((CHUNK_W, 4 * P), BF16),
            pltpu.VMEM((4 * P, CHUNK_W), BF16),
            pltpu.VMEM((C * C, LANES), F32),
            pltpu.VMEM((nblk * C, TOK_BLOCK), F32),
            pltpu.VMEM((nblk * C, TOK_BLOCK), F32),
        ] + [pltpu.VMEM((2 * nblk, 2 * P), F32)] * 2 + [pltpu.VMEM((nblk, 2 * P), F32)] * 6,
        compiler_params=_cparams(("parallel",), 40 << 20),
        name="ssm",
    )(*uts, rowp, colp, bt(b_re), bt(b_im), cr(c_re), cr(c_im), ct(c_re), ct(c_im), dcol)


def _merge_kernel(x_ref, a_ref, yt_ref, gate_ref, wglu_ref, bglu_ref, wao_ref, wso_ref, wout_ref, o_ref):
    d = x_ref.shape[1]
    y = jnp.concatenate([yt_ref[j].astype(F32).T for j in range(yt_ref.shape[0])], axis=0)
    z = jax.nn.gelu(y)
    z = z * jax.nn.sigmoid(_dot(z.astype(BF16), wglu_ref[...]) + bglu_ref[...])
    m = _dot(z.astype(BF16), wso_ref[...])
    a = _dot(a_ref[...], wao_ref[...])
    g_attn, g_ssm = gate_ref[:, 0:d].astype(F32), gate_ref[:, d:2 * d].astype(F32)
    merged = jax.nn.sigmoid(g_attn) * a + jax.nn.sigmoid(g_ssm) * m
    o_ref[...] = x_ref[...] + _dot(merged.astype(BF16), wout_ref[...])


def _merge(x, attn, yt, gates, w_glu, b_glu, w_ao, w_so, w_out):
    t, d = x.shape
    aw, sw = attn.shape[1], yt.shape[1]
    tm = MERGE_ROWS
    weights = (sw * sw + aw * d + sw * d + d * d) * 2
    vmem = weights + 2 * tm * (2 * d * 4 + (aw + sw + 2 * d) * 2) + 6 * tm * d * 4 + (4 << 20)
    return pl.pallas_call(
        _merge_kernel,
        out_shape=jax.ShapeDtypeStruct((t, d), F32),
        grid=(t // tm,),
        in_specs=[
            pl.BlockSpec((tm, d), lambda i: (i, 0)),
            pl.BlockSpec((tm, aw), lambda i: (i, 0)),
            pl.BlockSpec((tm // TOK_BLOCK, sw, TOK_BLOCK), lambda i: (i, 0, 0)),
            pl.BlockSpec((tm, 2 * d), lambda i: (i, 0)),
            _const_spec((sw, sw)), _const_spec((1, sw)), _const_spec((aw, d)), _const_spec((sw, d)),
            _const_spec((d, d)),
        ],
        out_specs=pl.BlockSpec((tm, d), lambda i: (i, 0)),
        compiler_params=_cparams(("parallel",), vmem),
        name="mixer_out",
    )(x, attn, yt, gates, w_glu, b_glu.reshape(1, sw), w_ao, w_so, w_out)


def _layer(xs, ffn1_norm, ffn1_w_gate_up, ffn1_w_down, mix_norm, w_in, q_norm, k_norm, attn_sink,
           ssm_lambda_re, ssm_lambda_im, ssm_log_dt, ssm_b_re, ssm_b_im, ssm_c_re, ssm_c_im, ssm_d,
           w_glu, b_glu, w_attn_out, w_ssm_out, w_out, ffn2_norm, ffn2_w_gate_up, ffn2_w_down):
    d = xs[0].shape[-1]
    seqs = [(x.shape[0], x.shape[1]) for x in xs]
    aw = N_Q_HEADS * HEAD_DIM
    kw = N_KV_HEADS * HEAD_DIM
    sw = ssm_d.shape[0]

    o_u, o_g = aw + 2 * kw, aw + 2 * kw + sw
    w_qkv = w_in[:, :o_u].astype(BF16)
    w_u_t = w_in[:, o_u:o_g].T.astype(BF16)
    w_gates = w_in[:, o_g:].astype(BF16)
    w1_gu, w1_d = _ffn_weights(ffn1_w_gate_up, ffn1_w_down)
    w2_gu, w2_d = _ffn_weights(ffn2_w_gate_up, ffn2_w_down)
    w_glu_b, w_ao, w_so, w_o = (w.astype(BF16) for w in (w_glu, w_attn_out, w_ssm_out, w_out))
    rope_tab = _rope_table(max(s for _, s in seqs))

    x1s, attns, gates, uts = [], [], [], []
    for x, (b, s) in zip(xs, seqs):
        x1 = _ffn(x.reshape(b * s, d), ffn1_norm, w1_gu, w1_d)
        qkv, ut, gate = _inproj(x1, mix_norm, w_qkv, w_u_t, w_gates, s, rope_tab, q_norm, k_norm)
        uts.append(ut)
        gates.append(gate)
        attns.append(_attention(qkv, b, s, attn_sink))
        x1s.append(x1)

    yts = _ssm(uts, seqs, ssm_lambda_re, ssm_lambda_im, ssm_log_dt, ssm_b_re, ssm_b_im, ssm_c_re, ssm_c_im, ssm_d)

    outs = []
    for x_in, x1, attn, yt, gate in zip(xs, x1s, attns, yts, gates):
        x2 = _merge(x1, attn, yt, gate, w_glu_b, b_glu, w_ao, w_so, w_o)
        outs.append(_ffn(x2, ffn2_norm, w2_gu, w2_d).reshape(x_in.shape))
    return tuple(outs)


def kernel(x_prompt, x_sample, ffn1_norm, ffn1_w_gate_up, ffn1_w_down, mix_norm, w_in, q_norm, k_norm, attn_sink, ssm_lambda_re, ssm_lambda_im, ssm_log_dt, ssm_b_re, ssm_b_im, ssm_c_re, ssm_c_im, ssm_d, w_glu, b_glu, w_attn_out, w_ssm_out, w_out, ffn2_norm, ffn2_w_gate_up, ffn2_w_down):
    xs = (x_prompt, x_sample)
    depth = ffn1_norm.shape[0]
    params = (ffn1_norm, ffn1_w_gate_up, ffn1_w_down, mix_norm, w_in, q_norm, k_norm, attn_sink,
              ssm_lambda_re, ssm_lambda_im, ssm_log_dt, ssm_b_re, ssm_b_im, ssm_c_re, ssm_c_im, ssm_d,
              w_glu, b_glu, w_attn_out, w_ssm_out, w_out, ffn2_norm, ffn2_w_gate_up, ffn2_w_down)
    for l in range(depth):
        xs = _layer(xs, *(p[l] for p in params))
    return xs
```
